```python
import jax, jax.numpy as jnp
from jax import lax
import numpy as np

D_MODEL = 4096
BATCH = 2
SEQ = 4096
DEPTH = 2
DEC_BATCH = 32
DEC_SEQ = 16
PAST_LEN = 1024

CHUNK = 64
GLA_HEADS = 4
GLA_QK = D_MODEL // 4
GLA_V = D_MODEL // 2
GLA_DK = GLA_QK // GLA_HEADS
GLA_DV = GLA_V // GLA_HEADS
GLA_LR = 16
GLA_TAU = 16.0
CONF_W = D_MODEL // 4
CONF_K = 31
SC_W = D_MODEL // 4
SC_K = 3
MIX_W = GLA_V + CONF_W + SC_W
IN_WIDTHS = (GLA_QK, GLA_QK, GLA_V, GLA_V, GLA_LR, CONF_W, CONF_W, SC_W, SC_W, SC_W)
IN_COLS = 2 * GLA_QK + 2 * GLA_V + GLA_LR + 2 * CONF_W + 3 * SC_W
PEER_HEADS = 8
PEER_NKEYS = 128
PEER_EXPERTS = PEER_NKEYS * PEER_NKEYS
PEER_TOPK = 16
PEER_DQ = 256
PEER_BLOCK = 128
PLE_DIM = 256
EPS = 1e-6

kernel_name = "hybrid_gla_conformer_shortconv_peer_stream_step"


def rms_norm(x, g):
    xf = x.astype(jnp.float32)
    y = xf * lax.rsqrt(jnp.mean(xf * xf, axis=-1, keepdims=True) + EPS)
    return (y * g.astype(jnp.float32)).astype(x.dtype)


def layer_norm(x, g, b):
    xf = x.astype(jnp.float32)
    mu = jnp.mean(xf, axis=-1, keepdims=True)
    var = jnp.mean(jnp.square(xf - mu), axis=-1, keepdims=True)
    y = (xf - mu) * lax.rsqrt(var + EPS)
    return (y * g.astype(jnp.float32) + b.astype(jnp.float32)).astype(x.dtype)


def causal_depthwise_conv(hist, x, w):
    k = w.shape[0]
    full = jnp.concatenate([hist.astype(x.dtype), x], axis=1)
    y = lax.conv_general_dilated(full, w[:, None, :].astype(x.dtype), window_strides=(1,), padding="VALID",
                                 dimension_numbers=("NWC", "WIO", "NWC"), feature_group_count=x.shape[-1])
    return y, full[:, full.shape[1] - (k - 1):, :]


def gla_recurrence(q, k, v, la, s0):
    b_, t_, h_, _ = q.shape
    dv = v.shape[-1]
    blk = min(CHUNK, t_)
    n = t_ // blk

    def to_blocks(a):
        return a.reshape(b_, n, blk, h_, a.shape[-1]).transpose(1, 0, 3, 2, 4)

    causal = jnp.tril(jnp.ones((blk, blk), dtype=bool))
    mid = (blk - 1) // 2

    def step(s, inp):
        qb, kb, vb, lb = inp
        cb = lax.cumsum(lb, axis=2)
        c_last = cb[:, :, -1:, :]
        c_mid = cb[:, :, mid:mid + 1, :]
        att = jnp.einsum("bhld,bhmd->bhlm", qb * jnp.exp(cb - c_mid), kb * jnp.exp(c_mid - cb))
        att = jnp.where(causal, att, 0.0)
        o = (jnp.einsum("bhlm,bhme->bhle", att, vb)
             + jnp.einsum("bhld,bhde->bhle", qb * jnp.exp(cb), s))
        s_new = (jnp.exp(c_last[:, :, 0, :])[..., None] * s
                 + jnp.einsum("bhld,bhle->bhde", kb * jnp.exp(c_last - cb), vb))
        return s_new, o

    s_fin, o = lax.scan(step, s0, (to_blocks(q), to_blocks(k), to_blocks(v), to_blocks(la)))
    o = o.transpose(1, 0, 3, 2, 4).reshape(b_, t_, h_, dv)
    return o, s_fin


def token_mixers(xn, s_gla, s_conf, s_sc, w_in, w_lr, b_lr, g_gla, w_dw_conf, b_dw_conf, ln_g, ln_b,
                 w_dw_sc, w_out):
    b_, t_, _ = xn.shape
    f32 = jnp.float32
    split_points = np.cumsum(np.array(IN_WIDTHS))[:-1].tolist()
    proj = xn @ w_in
    q, k, v, g, lr, ca, cg, sb, scg, sh = jnp.split(proj, split_points, axis=-1)

    la = jax.nn.log_sigmoid((lr @ w_lr + b_lr).astype(f32)) / GLA_TAU
    qh = q.astype(f32).reshape(b_, t_, GLA_HEADS, GLA_DK) * (GLA_DK ** -0.5)
    kh = k.astype(f32).reshape(b_, t_, GLA_HEADS, GLA_DK)
    vh = v.astype(f32).reshape(b_, t_, GLA_HEADS, GLA_DV)
    lah = la.reshape(b_, t_, GLA_HEADS, GLA_DK)
    o, s_gla_new = gla_recurrence(qh, kh, vh, lah, s_gla.astype(f32))
    o = o * lax.rsqrt(jnp.mean(o * o, axis=-1, keepdims=True) + EPS)
    o = o.reshape(b_, t_, GLA_V) * g_gla.astype(f32)
    y_gla = (o * jax.nn.silu(g.astype(f32))).astype(xn.dtype)

    u = ca * jax.nn.sigmoid(cg)
    c, s_conf_new = causal_depthwise_conv(s_conf, u, w_dw_conf)
    y_conf = jax.nn.silu(layer_norm(c + b_dw_conf, ln_g, ln_b))

    z, s_sc_new = causal_depthwise_conv(s_sc, scg * sh, w_dw_sc)
    y_sc = sb * z

    y = jnp.concatenate([y_gla, y_conf, y_sc], axis=-1) @ w_out
    return y, s_gla_new.astype(xn.dtype), s_conf_new, s_sc_new


def peer_ffn(xn, w_q, sub_k1, sub_k2, u_tab, v_tab):
    lead = xn.shape[:-1]
    x2 = xn.reshape(-1, D_MODEL)
    n_tok = x2.shape[0]
    n_blk = -(-n_tok // PEER_BLOCK)
    x2 = jnp.pad(x2, ((0, n_blk * PEER_BLOCK - n_tok), (0, 0))).reshape(n_blk, PEER_BLOCK, D_MODEL)
    half = PEER_DQ // 2

    def block_fn(xb):
        q = (xb @ w_q).reshape(PEER_BLOCK, PEER_HEADS, 2, half)
        s1 = jnp.einsum("thd,hnd->thn", q[:, :, 0], sub_k1).astype(jnp.float32)
        s2 = jnp.einsum("thd,hnd->thn", q[:, :, 1], sub_k2).astype(jnp.float32)
        v1, i1 = lax.top_k(s1, PEER_TOPK)
        v2, i2 = lax.top_k(s2, PEER_TOPK)
        cand = (v1[..., :, None] + v2[..., None, :]).reshape(PEER_BLOCK, PEER_HEADS, PEER_TOPK * PEER_TOPK)
        cidx = (i1[..., :, None] * PEER_NKEYS + i2[..., None, :]).reshape(PEER_BLOCK, PEER_HEADS, PEER_TOPK * PEER_TOPK)
        sc, pos = lax.top_k(cand, PEER_TOPK)
        eidx = jnp.take_along_axis(cidx, pos, axis=-1)
        gate = jax.nn.softmax(sc, axis=-1)
        act = jax.nn.gelu(jnp.einsum("thkd,td->thk", u_tab[eidx], xb).astype(jnp.float32))
        w = (gate * act).astype(xb.dtype)
        return jnp.einsum("thk,thkd->td", w, v_tab[eidx])

    out = lax.map(block_fn, x2).reshape(n_blk * PEER_BLOCK, D_MODEL)[:n_tok]
    return out.reshape(*lead, D_MODEL)


def run_trunk(x, p, s_gla, s_conf, s_sc, g_mix, w_in, gla_w_lr, gla_b_lr, gla_g_out, conf_w_dw, conf_b_dw,
              conf_ln_g, conf_ln_b, sc_w_dw, w_out, g_ffn, peer_w_q, peer_k1, peer_k2, peer_u, peer_v,
              g_ple, ple_w_gate, ple_w_proj, g_final):
    h = x
    new_gla, new_conf, new_sc = [], [], []
    for i in range(DEPTH):
        y, sg, scf, ssc = token_mixers(rms_norm(h, g_mix[i]), s_gla[i], s_conf[i], s_sc[i], w_in[i],
                                       gla_w_lr[i], gla_b_lr[i], gla_g_out[i], conf_w_dw[i], conf_b_dw[i],
                                       conf_ln_g[i], conf_ln_b[i], sc_w_dw[i], w_out[i])
        h = h + y
        h = h + peer_ffn(rms_norm(h, g_ffn[i]), peer_w_q[i], peer_k1[i], peer_k2[i], peer_u[i], peer_v[i])
        gate = jax.nn.sigmoid(rms_norm(h, g_ple[i]) @ ple_w_gate[i])
        h = h + (p[i] @ ple_w_proj[i]) * gate
        new_gla.append(sg)
        new_conf.append(scf)
        new_sc.append(ssc)
    return rms_norm(h, g_final), jnp.stack(new_gla), jnp.stack(new_conf), jnp.stack(new_sc)


def setup_inputs(seed: int = 0) -> dict:
    key = jax.random.key(seed)
    ks = jax.random.split(key, 32)
    f32 = jnp.float32

    def nrm(k, shape, scale):
        return jax.random.normal(k, shape, f32) * scale

    return {
        "x_prompt": nrm(ks[0], (BATCH, SEQ, D_MODEL), 1.0),
        "x_sample": nrm(ks[1], (DEC_BATCH, DEC_SEQ, D_MODEL), 1.0),
        "state_gla": nrm(ks[2], (DEPTH, DEC_BATCH, GLA_HEADS, GLA_DK, GLA_DV), 1.0),
        "state_conf": nrm(ks[3], (DEPTH, DEC_BATCH, CONF_K - 1, CONF_W), 0.5),
        "state_sconv": nrm(ks[4], (DEPTH, DEC_BATCH, SC_K - 1, SC_W), 0.5),
        "p_prompt": nrm(ks[5], (DEPTH, BATCH, SEQ, PLE_DIM), 1.0),
        "p_sample": nrm(ks[6], (DEPTH, DEC_BATCH, DEC_SEQ, PLE_DIM), 1.0),
        "g_mix": 1.0 + nrm(ks[7], (DEPTH, D_MODEL), 0.01),
        "w_in": nrm(ks[8], (DEPTH, D_MODEL, IN_COLS), D_MODEL ** -0.5),
        "gla_w_lr": nrm(ks[9], (DEPTH, GLA_LR, GLA_QK), GLA_LR ** -0.5),
        "gla_b_lr": nrm(ks[10], (DEPTH, GLA_QK), 0.01),
        "gla_g_out": 1.0 + nrm(ks[11], (DEPTH, GLA_V), 0.01),
        "conf_w_dw": nrm(ks[12], (DEPTH, CONF_K, CONF_W), CONF_K ** -0.5),
        "conf_b_dw": nrm(ks[13], (DEPTH, CONF_W), 0.01),
        "conf_ln_g": 1.0 + nrm(ks[14], (DEPTH, CONF_W), 0.01),
        "conf_ln_b": nrm(ks[15], (DEPTH, CONF_W), 0.01),
        "sc_w_dw": nrm(ks[16], (DEPTH, SC_K, SC_W), SC_K ** -0.5),
        "w_out": nrm(ks[17], (DEPTH, MIX_W, D_MODEL), MIX_W ** -0.5),
        "g_ffn": 1.0 + nrm(ks[18], (DEPTH, D_MODEL), 0.01),
        "peer_w_q": nrm(ks[19], (DEPTH, D_MODEL, PEER_HEADS * PEER_DQ), D_MODEL ** -0.5),
        "peer_k1": nrm(ks[20], (DEPTH, PEER_HEADS, PEER_NKEYS, PEER_DQ // 2), (PEER_DQ // 2) ** -0.5),
        "peer_k2": nrm(ks[21], (DEPTH, PEER_HEADS, PEER_NKEYS, PEER_DQ // 2), (PEER_DQ // 2) ** -0.5),
        "peer_u": nrm(ks[22], (DEPTH, PEER_EXPERTS, D_MODEL), D_MODEL ** -0.5),
        "peer_v": nrm(ks[23], (DEPTH, PEER_EXPERTS, D_MODEL), PEER_HEADS ** -0.5),
        "g_ple": 1.0 + nrm(ks[24], (DEPTH, D_MODEL), 0.01),
        "ple_w_gate": nrm(ks[25], (DEPTH, D_MODEL, D_MODEL), D_MODEL ** -0.5),
        "ple_w_proj": nrm(ks[26], (DEPTH, PLE_DIM, D_MODEL), PLE_DIM ** -0.5),
        "g_final": 1.0 + nrm(ks[27], (D_MODEL,), 0.01),
    }


def reference(x_prompt, x_sample, state_gla, state_conf, state_sconv, p_prompt, p_sample, g_mix, w_in,
              gla_w_lr, gla_b_lr, gla_g_out, conf_w_dw, conf_b_dw, conf_ln_g, conf_ln_b, sc_w_dw, w_out,
              g_ffn, peer_w_q, peer_k1, peer_k2, peer_u, peer_v, g_ple, ple_w_gate, ple_w_proj, g_final):
    bp = x_prompt.shape[0]
    dt = x_prompt.dtype
    z_gla = jnp.zeros((DEPTH, bp, GLA_HEADS, GLA_DK, GLA_DV), dt)
    z_conf = jnp.zeros((DEPTH, bp, CONF_K - 1, CONF_W), dt)
    z_sc = jnp.zeros((DEPTH, bp, SC_K - 1, SC_W), dt)
    y_prompt, gla_p, conf_p, sc_p = run_trunk(
        x_prompt, p_prompt, z_gla, z_conf, z_sc, g_mix, w_in, gla_w_lr, gla_b_lr, gla_g_out, conf_w_dw,
        conf_b_dw, conf_ln_g, conf_ln_b, sc_w_dw, w_out, g_ffn, peer_w_q, peer_k1, peer_k2, peer_u, peer_v,
        g_ple, ple_w_gate, ple_w_proj, g_final)
    y_sample, gla_s, conf_s, sc_s = run_trunk(
        x_sample, p_sample, state_gla, state_conf, state_sconv, g_mix, w_in, gla_w_lr, gla_b_lr, gla_g_out,
        conf_w_dw, conf_b_dw, conf_ln_g, conf_ln_b, sc_w_dw, w_out, g_ffn, peer_w_q, peer_k1, peer_k2, peer_u,
        peer_v, g_ple, ple_w_gate, ple_w_proj, g_final)
    return (y_prompt, y_sample, gla_p, conf_p, sc_p, gla_s, conf_s, sc_s)
```

```python
import functools

import jax
import jax.numpy as jnp
from jax import lax
from jax.experimental import pallas as pl
from jax.experimental.pallas import tpu as pltpu

F32 = jnp.float32
BF16 = jnp.bfloat16

EPS = 1e-6
CHUNK = 64
GLA_HEADS = 4
GLA_LR = 16
GLA_TAU = 16.0
CONF_K = 31
SC_K = 3
PEER_HEADS = 8
PEER_NKEYS = 128
PEER_TOPK = 16

LANES = 128
SUBLANES = 8
CONF_HIST = 32
SC_HIST = 8
NORM_ROWS = 128
GLA_ROWS = 512
CONV_ROWS = 256
A_PER_BLOCK = 4
NOT_SELECTED = 64.0
MIB = 1024 * 1024

_CELLS = [(i, j) for i in range(PEER_TOPK) for j in range(PEER_TOPK) if (i + 1) * (j + 1) <= PEER_TOPK]


def _params(semantics, vmem_mib):
    return pltpu.CompilerParams(dimension_semantics=semantics, vmem_limit_bytes=vmem_mib * MIB)


def _rms(x, g):
    ms = jnp.mean(x * x, axis=-1, keepdims=True)
    return x * lax.rsqrt(ms + EPS) * g


def _rms_row_groups(h_ref, g_ref, store):
    g = g_ref[...]
    for r0 in range(0, h_ref.shape[0], NORM_ROWS):
        rows = slice(r0, min(r0 + NORM_ROWS, h_ref.shape[0]))
        store(rows, _rms(h_ref[rows, :], g))


def _in_proj_kernel(h_ref, g_ref, w_ref, wlr_in_ref, wlr_ref, blr_ref, proj_ref, la_ref, xn_ref):
    @pl.when(pl.program_id(1) == 0)
    def _():
        def store(rows, xn):
            xn_ref[rows, :] = xn.astype(BF16)

        _rms_row_groups(h_ref, g_ref, store)
        lr = jnp.dot(xn_ref[...], wlr_in_ref[...], preferred_element_type=F32)
        z = jnp.dot(lr.astype(BF16), wlr_ref[...], preferred_element_type=F32) + blr_ref[...]
        la_ref[...] = (jnp.minimum(z, 0.0) - jnp.log1p(jnp.exp(-jnp.abs(z)))) * (1.0 / GLA_TAU)

    proj_ref[...] = jnp.dot(xn_ref[...], w_ref[...], preferred_element_type=F32)


def _in_proj(h, g, w_main, w_lr_in, w_lr, b_lr):
    n, d = h.shape
    ncols = w_main.shape[1]
    qk = w_lr.shape[1]
    bm = min(512, n)
    bn = d // 8
    return pl.pallas_call(
        _in_proj_kernel,
        grid=(n // bm, ncols // bn),
        in_specs=[
            pl.BlockSpec((bm, d), lambda i, j: (i, 0)),
            pl.BlockSpec((1, d), lambda i, j: (0, 0)),
            pl.BlockSpec((d, bn), lambda i, j: (0, j)),
            pl.BlockSpec((d, LANES), lambda i, j: (0, 0)),
            pl.BlockSpec((LANES, qk), lambda i, j: (0, 0)),
            pl.BlockSpec((1, qk), lambda i, j: (0, 0)),
        ],
        out_specs=[
            pl.BlockSpec((bm, bn), lambda i, j: (i, j)),
            pl.BlockSpec((bm, qk), lambda i, j: (i, 0)),
        ],
        out_shape=[jax.ShapeDtypeStruct((n, ncols), F32), jax.ShapeDtypeStruct((n, qk), F32)],
        scratch_shapes=[pltpu.VMEM((bm, d), BF16)],
        compiler_params=_params(("parallel", "arbitrary"), 56),
        name="in_proj",
    )(h, g, w_main, w_lr_in, w_lr, b_lr)


def _gla_kernel(*refs, chunk, n_chunks, has_state, scale):
    if has_state:
        q_ref, k_ref, v_ref, g_ref, la_ref, gg_ref, s0_ref, y_ref, sout_ref, st_ref = refs
    else:
        q_ref, k_ref, v_ref, g_ref, la_ref, gg_ref, y_ref, sout_ref, st_ref = refs
    tt = pl.program_id(2)

    @pl.when(tt == 0)
    def _():
        if has_state:
            st_ref[...] = s0_ref[0, 0].T
        else:
            st_ref[...] = jnp.zeros_like(st_ref)

    row = lax.broadcasted_iota(jnp.int32, (chunk, chunk), 0)
    col = lax.broadcasted_iota(jnp.int32, (chunk, chunk), 1)
    causal = row >= col
    tril = jnp.where(causal, 1.0, 0.0).astype(BF16)
    mid = (chunk - 1) // 2
    nt = (((1,), (1,)), ((), ()))
    tn = (((0,), (0,)), ((), ()))

    for c in range(n_chunks):
        sl = pl.ds(c * chunk, chunk)
        la = la_ref[sl, :]
        la_hi = la.astype(BF16)
        r1 = la - la_hi.astype(F32)
        la_mid = r1.astype(BF16)
        la_lo = (r1 - la_mid.astype(F32)).astype(BF16)
        cb = (jnp.dot(tril, la_hi, preferred_element_type=F32)
              + jnp.dot(tril, la_mid, preferred_element_type=F32)
              + jnp.dot(tril, la_lo, preferred_element_type=F32))
        c_last = cb[chunk - 1:chunk, :]
        c_mid = cb[mid:mid + 1, :]
        q = q_ref[sl, :] * scale
        k = k_ref[sl, :]
        vb = v_ref[sl, :].astype(BF16)
        st = st_ref[...]
        qe = (q * jnp.exp(cb - c_mid)).astype(BF16)
        ke = (k * jnp.exp(c_mid - cb)).astype(BF16)
        att = lax.dot_general(qe, ke, nt, preferred_element_type=F32)
        att = jnp.where(causal, att, 0.0).astype(BF16)
        q2 = (q * jnp.exp(cb)).astype(BF16)
        o = (jnp.dot(att, vb, preferred_element_type=F32)
             + lax.dot_general(q2, st.astype(BF16), nt, preferred_element_type=F32))
        k2 = (k * jnp.exp(c_last - cb)).astype(BF16)
        st_ref[...] = jnp.exp(c_last) * st + lax.dot_general(vb, k2, tn, preferred_element_type=F32)
        o = o * lax.rsqrt(jnp.mean(o * o, axis=-1, keepdims=True) + EPS)
        gate = g_ref[sl, :]
        y_ref[sl, :] = (o * gg_ref[...] * (gate * jax.nn.sigmoid(gate))).astype(y_ref.dtype)

    @pl.when(tt == pl.num_programs(2) - 1)
    def _():
        sout_ref[0, 0] = st_ref[...].T


def _gla(proj, la, gg, s0, batch, seq):
    n = proj.shape[0]
    qk = la.shape[1]
    v_w = gg.shape[1]
    dk = qk // GLA_HEADS
    dv = v_w // GLA_HEADS
    chunk = min(CHUNK, seq)
    tb = min(seq, GLA_ROWS)
    assert seq % tb == 0 and tb % chunk == 0
    nt = seq // tb
    has_state = s0 is not None
    k_blk = qk // dk
    v_blk = 2 * qk // dv
    g_blk = (2 * qk + v_w) // dv
    rows = lambda b, h, t: b * nt + t
    in_specs = [
        pl.BlockSpec((tb, dk), lambda b, h, t: (rows(b, h, t), h)),
        pl.BlockSpec((tb, dk), lambda b, h, t: (rows(b, h, t), k_blk + h)),
        pl.BlockSpec((tb, dv), lambda b, h, t: (rows(b, h, t), v_blk + h)),
        pl.BlockSpec((tb, dv), lambda b, h, t: (rows(b, h, t), g_blk + h)),
        pl.BlockSpec((tb, dk), lambda b, h, t: (rows(b, h, t), h)),
        pl.BlockSpec((1, dv), lambda b, h, t: (0, h)),
    ]
    args = [proj, proj, proj, proj, la, gg]
    if has_state:
        in_specs.append(pl.BlockSpec((1, 1, dk, dv), lambda b, h, t: (b, h, 0, 0)))
        args.append(s0)
    return pl.pallas_call(
        functools.partial(_gla_kernel, chunk=chunk, n_chunks=tb // chunk, has_state=has_state, scale=dk ** -0.5),
        grid=(batch, GLA_HEADS, nt),
        in_specs=in_specs,
        out_specs=[
            pl.BlockSpec((tb, dv), lambda b, h, t: (rows(b, h, t), h)),
            pl.BlockSpec((1, 1, dk, dv), lambda b, h, t: (b, h, 0, 0)),
        ],
        out_shape=[jax.ShapeDtypeStruct((n, v_w), BF16),
                   jax.ShapeDtypeStruct((batch, GLA_HEADS, dk, dv), F32)],
        scratch_shapes=[pltpu.VMEM((dv, dk), F32)],
        compiler_params=_params(("parallel", "parallel", "arbitrary"), 32),
        name="gla",
    )(*args)


def _conv_kernel(*refs, tb, has_state):
    if has_state:
        (ca_ref, cg_ref, sb_ref, scg_ref, sh_ref, wc_ref, bc_ref, lg_ref, lb_ref, ws_ref, hc_ref, hs_ref,
         yc_ref, ys_ref, nc_ref, ns_ref, ubuf, zbuf) = refs
    else:
        (ca_ref, cg_ref, sb_ref, scg_ref, sh_ref, wc_ref, bc_ref, lg_ref, lb_ref, ws_ref,
         yc_ref, ys_ref, nc_ref, ns_ref, ubuf, zbuf) = refs
    tt = pl.program_id(1)
    c_skip = CONF_HIST - (CONF_K - 1)
    s_skip = SC_HIST - (SC_K - 1)

    @pl.when(tt == 0)
    def _():
        ubuf[0:CONF_HIST, :] = jnp.zeros((CONF_HIST, ubuf.shape[1]), F32)
        zbuf[0:SC_HIST, :] = jnp.zeros((SC_HIST, zbuf.shape[1]), F32)
        if has_state:
            ubuf[c_skip:CONF_HIST, :] = hc_ref[0]
            zbuf[s_skip:SC_HIST, :] = hs_ref[0]

    @pl.when(tt > 0)
    def _():
        ubuf[0:CONF_HIST, :] = ubuf[tb:tb + CONF_HIST, :]
        zbuf[0:SC_HIST, :] = zbuf[tb:tb + SC_HIST, :]

    ubuf[CONF_HIST:CONF_HIST + tb, :] = ca_ref[...] * jax.nn.sigmoid(cg_ref[...])
    zbuf[SC_HIST:SC_HIST + tb, :] = scg_ref[...] * sh_ref[...]

    acc = wc_ref[0:1, :] * ubuf[c_skip:c_skip + tb, :]
    for j in range(1, CONF_K):
        acc = acc + wc_ref[j:j + 1, :] * ubuf[c_skip + j:c_skip + j + tb, :]
    c = acc + bc_ref[...]
    mu = jnp.mean(c, axis=-1, keepdims=True)
    cc = c - mu
    var = jnp.mean(cc * cc, axis=-1, keepdims=True)
    y = cc * lax.rsqrt(var + EPS) * lg_ref[...] + lb_ref[...]
    yc_ref[...] = (y * jax.nn.sigmoid(y)).astype(yc_ref.dtype)

    z = ws_ref[0:1, :] * zbuf[s_skip:s_skip + tb, :]
    for j in range(1, SC_K):
        z = z + ws_ref[j:j + 1, :] * zbuf[s_skip + j:s_skip + j + tb, :]
    ys_ref[...] = (sb_ref[...] * z).astype(ys_ref.dtype)

    @pl.when(tt == pl.num_programs(1) - 1)
    def _():
        nc_ref[0] = ubuf[tb + c_skip:tb + CONF_HIST, :]
        ns_ref[0] = zbuf[tb + s_skip:tb + SC_HIST, :]


def _convs(proj, hc, hs, wc, bc, lg, lb, ws, batch, seq):
    n = proj.shape[0]
    w = wc.shape[1]
    tb = min(seq, CONV_ROWS)
    nt = seq // tb
    assert seq % tb == 0 and (nt == 1 or tb >= CONF_HIST)
    has_state = hc is not None
    first = (proj.shape[1] - 5 * w) // w
    rows = lambda b, t: b * nt + t
    col = lambda c: pl.BlockSpec((tb, w), lambda b, t: (rows(b, t), first + c))
    vec = lambda r: pl.BlockSpec((r, w), lambda b, t: (0, 0))
    in_specs = [col(0), col(1), col(2), col(3), col(4), vec(CONF_K), vec(1), vec(1), vec(1), vec(SC_K)]
    args = [proj] * 5 + [wc, bc, lg, lb, ws]
    if has_state:
        in_specs += [pl.BlockSpec((1, CONF_K - 1, w), lambda b, t: (b, 0, 0)),
                     pl.BlockSpec((1, SC_K - 1, w), lambda b, t: (b, 0, 0))]
        args += [hc, hs]
    return pl.pallas_call(
        functools.partial(_conv_kernel, tb=tb, has_state=has_state),
        grid=(batch, nt),
        in_specs=in_specs,
        out_specs=[
            pl.BlockSpec((tb, w), lambda b, t: (rows(b, t), 0)),
            pl.BlockSpec((tb, w), lambda b, t: (rows(b, t), 0)),
            pl.BlockSpec((1, CONF_K - 1, w), lambda b, t: (b, 0, 0)),
            pl.BlockSpec((1, SC_K - 1, w), lambda b, t: (b, 0, 0)),
        ],
        out_shape=[jax.ShapeDtypeStruct((n, w), BF16), jax.ShapeDtypeStruct((n, w), BF16),
                   jax.ShapeDtypeStruct((batch, CONF_K - 1, w), F32),
                   jax.ShapeDtypeStruct((batch, SC_K - 1, w), F32)],
        scratch_shapes=[pltpu.VMEM((CONF_HIST + tb, w), F32), pltpu.VMEM((SC_HIST + tb, w), F32)],
        compiler_params=_params(("parallel", "arbitrary"), 32),
        name="convs",
    )(*args)


def _out_proj_kernel(yg_ref, yc_ref, ys_ref, w_ref, h_ref, o_ref):
    v_w = yg_ref.shape[1]
    c_w = yc_ref.shape[1]
    acc = jnp.dot(yg_ref[...], w_ref[0:v_w, :], preferred_element_type=F32)
    acc = acc + jnp.dot(yc_ref[...], w_ref[v_w:v_w + c_w, :], preferred_element_type=F32)
    acc = acc + jnp.dot(ys_ref[...], w_ref[v_w + c_w:, :], preferred_element_type=F32)
    o_ref[...] = h_ref[...] + acc


def _out_proj(yg, yc, ys, w_out, h):
    n, d = h.shape
    bm = min(1024, n)
    bn = 512
    return pl.pallas_call(
        _out_proj_kernel,
        grid=(n // bm, d // bn),
        in_specs=[
            pl.BlockSpec((bm, yg.shape[1]), lambda i, j: (i, 0)),
            pl.BlockSpec((bm, yc.shape[1]), lambda i, j: (i, 0)),
            pl.BlockSpec((bm, ys.shape[1]), lambda i, j: (i, 0)),
            pl.BlockSpec((w_out.shape[0], bn), lambda i, j: (0, j)),
            pl.BlockSpec((bm, bn), lambda i, j: (i, j)),
        ],
        out_specs=pl.BlockSpec((bm, bn), lambda i, j: (i, j)),
        out_shape=jax.ShapeDtypeStruct((n, d), F32),
        compiler_params=_params(("parallel", "arbitrary"), 48),
        name="out_proj",
    )(yg, yc, ys, w_out, h)


def _scores_kernel(h_ref, g_ref, wq_ref, k1_ref, k2_ref, xnt_ref, s1_ref, s2_ref):
    @pl.when(pl.program_id(1) == 0)
    def _():
        def store(rows, xn):
            xnt_ref[:, rows] = xn.T.astype(BF16)

        _rms_row_groups(h_ref, g_ref, store)

    q = jnp.dot(wq_ref[...], xnt_ref[...], preferred_element_type=F32)
    half = q.shape[0] // 2
    s1_ref[0] = jnp.dot(k1_ref[0], q[:half].astype(BF16), preferred_element_type=F32)
    s2_ref[0] = jnp.dot(k2_ref[0], q[half:].astype(BF16), preferred_element_type=F32)


def _scores(h, g, wq_t, k1, k2):
    n, d = h.shape
    bt = min(512, n)
    dq = wq_t.shape[0] // PEER_HEADS
    nk = k1.shape[1]
    return pl.pallas_call(
        _scores_kernel,
        grid=(n // bt, PEER_HEADS),
        in_specs=[
            pl.BlockSpec((bt, d), lambda i, hd: (i, 0)),
            pl.BlockSpec((1, d), lambda i, hd: (0, 0)),
            pl.BlockSpec((dq, d), lambda i, hd: (hd, 0)),
            pl.BlockSpec((1, nk, dq // 2), lambda i, hd: (hd, 0, 0)),
            pl.BlockSpec((1, nk, dq // 2), lambda i, hd: (hd, 0, 0)),
        ],
        out_specs=[
            pl.BlockSpec((d, bt), lambda i, hd: (0, i)),
            pl.BlockSpec((1, nk, bt), lambda i, hd: (hd, 0, i)),
            pl.BlockSpec((1, nk, bt), lambda i, hd: (hd, 0, i)),
        ],
        out_shape=[jax.ShapeDtypeStruct((d, n), BF16),
                   jax.ShapeDtypeStruct((PEER_HEADS, nk, n), F32),
                   jax.ShapeDtypeStruct((PEER_HEADS, nk, n), F32)],
        compiler_params=_params(("parallel", "arbitrary"), 48),
        name="peer_scores",
    )(h, g, wq_t, k1, k2)


def _rank_keys(s_ref, s_scr, r_scr, v_scr):
    nh = PEER_HEADS
    nk = PEER_NKEYS
    bt = s_ref.shape[1]
    m = jnp.full((nh, bt), -jnp.inf, F32)
    for a in range(nk):
        t = s_ref[pl.ds(a, nh, stride=nk), :]
        s_scr[a * nh:(a + 1) * nh, :] = t
        r_scr[a * nh:(a + 1) * nh, :] = jnp.full((nh, bt), NOT_SELECTED, F32)
        m = jnp.maximum(m, t)

    def tile(a):
        return pl.ds(pl.multiple_of(a * nh, nh), nh)

    def round_body(r, m):
        def find(a, idx):
            return jnp.minimum(idx, jnp.where(s_scr[tile(a), :] == m, a.astype(F32), float(nk)))

        idx = lax.fori_loop(0, nk, find, jnp.full((nh, bt), float(nk), F32), unroll=8)
        rank = r.astype(F32)

        def knock(a, m_next):
            hit = idx == a.astype(F32)
            s_new = jnp.where(hit, -jnp.inf, s_scr[tile(a), :])
            s_scr[tile(a), :] = s_new
            r_scr[tile(a), :] = jnp.where(hit, rank, r_scr[tile(a), :])
            return jnp.maximum(m_next, s_new)

        m_next = lax.fori_loop(0, nk, knock, jnp.full((nh, bt), -jnp.inf, F32), unroll=8)
        v_scr[r] = m
        return m_next

    lax.fori_loop(0, PEER_TOPK, round_body, m)


def _topk_kernel(s1_ref, s2_ref, c1_ref, e1_ref, r2_ref, e2_ref,
                 s_scr, r1_scr, r2_scr, v1_scr, v2_scr, c_scr, e1z_scr, e2_scr):
    nh = PEER_HEADS
    nk = PEER_NKEYS
    _rank_keys(s1_ref, s_scr, r1_scr, v1_scr)
    _rank_keys(s2_ref, s_scr, r2_scr, v2_scr)

    v1 = [v1_scr[i] for i in range(PEER_TOPK)]
    v2 = [v2_scr[j] for j in range(PEER_TOPK)]
    cand = {(i, j): v1[i] + v2[j] for (i, j) in _CELLS}
    before = {p: 0.0 for p in _CELLS}
    for x, p in enumerate(_CELLS):
        for q in _CELLS[x + 1:]:
            if p[0] <= q[0] and p[1] <= q[1]:
                before[q] = before[q] + 1.0
            elif not (q[0] <= p[0] and q[1] <= p[1]):
                p_first = jnp.where(cand[p] >= cand[q], 1.0, 0.0)
                before[q] = before[q] + p_first
                before[p] = before[p] + (1.0 - p_first)
    e1 = [jnp.exp(v1[i] - v1[0]) for i in range(PEER_TOPK)]
    e2 = [jnp.exp(v2[j] - v2[0]) for j in range(PEER_TOPK)]
    height = [jnp.zeros_like(v1[0]) for _ in range(PEER_TOPK)]
    z = jnp.zeros_like(v1[0])
    for (i, j) in _CELLS:
        sel = jnp.where(before[(i, j)] < float(PEER_TOPK), 1.0, 0.0)
        height[i] = height[i] + sel
        z = z + sel * (e1[i] * e2[j])
    inv_z = 1.0 / z
    for i in range(PEER_TOPK):
        c_scr[i] = height[i]
        e1z_scr[i] = e1[i] * inv_z
        e2_scr[i] = e2[i]

    def emit1(a, carry):
        rk = r1_scr[pl.ds(pl.multiple_of(a * nh, nh), nh), :]
        c = jnp.zeros_like(rk)
        e = jnp.zeros_like(rk)
        for r in range(PEER_TOPK):
            eq = rk == float(r)
            c = jnp.where(eq, c_scr[r], c)
            e = jnp.where(eq, e1z_scr[r], e)
        c1_ref[a] = c
        e1_ref[a] = e
        return carry

    lax.fori_loop(0, nk, emit1, 0)

    for hd in range(nh):
        rk = r2_scr[pl.ds(hd, nk, stride=nh), :]
        e = jnp.zeros_like(rk)
        for r in range(PEER_TOPK):
            e = jnp.where(rk == float(r), e2_scr[r, hd:hd + 1, :], e)
        r2_ref[hd] = rk
        e2_ref[hd] = e


def _topk(s1, s2):
    nh, nk, n = s1.shape
    bt = LANES
    rows = nh * nk
    big = pltpu.VMEM((rows, bt), F32)
    small = pltpu.VMEM((PEER_TOPK, nh, bt), F32)
    return pl.pallas_call(
        _topk_kernel,
        grid=(n // bt,),
        in_specs=[pl.BlockSpec((rows, bt), lambda i: (0, i)), pl.BlockSpec((rows, bt), lambda i: (0, i))],
        out_specs=[
            pl.BlockSpec((nk, nh, bt), lambda i: (0, 0, i)),
            pl.BlockSpec((nk, nh, bt), lambda i: (0, 0, i)),
            pl.BlockSpec((nh, nk, bt), lambda i: (0, 0, i)),
            pl.BlockSpec((nh, nk, bt), lambda i: (0, 0, i)),
        ],
        out_shape=[jax.ShapeDtypeStruct((nk, nh, n), F32), jax.ShapeDtypeStruct((nk, nh, n), F32),
                   jax.ShapeDtypeStruct((nh, nk, n), F32), jax.ShapeDtypeStruct((nh, nk, n), F32)],
        scratch_shapes=[big, big, big, small, small, small, small, small],
        compiler_params=_params(("parallel",), 40),
        name="peer_topk",
    )(s1.reshape(rows, n), s2.reshape(rows, n))


def _peer_ffn_kernel(xnt_ref, u_ref, vt_ref, c1_ref, e1_ref, r2_ref, e2_ref, out_ref, w_scr):
    nk = PEER_NKEYS
    act = jax.nn.gelu(jnp.dot(u_ref[...], xnt_ref[...], preferred_element_type=F32))
    for al in range(A_PER_BLOCK):
        c1 = c1_ref[al]
        e1 = e1_ref[al]
        g = jnp.zeros((nk, act.shape[1]), F32)
        for hd in range(PEER_HEADS):
            sel = r2_ref[hd] < c1[hd:hd + 1, :]
            g = g + jnp.where(sel, e1[hd:hd + 1, :] * e2_ref[hd], 0.0)
        w_scr[al * nk:(al + 1) * nk, :] = (g * act[al * nk:(al + 1) * nk, :]).astype(BF16)
    part = jnp.dot(vt_ref[...], w_scr[...], preferred_element_type=F32)

    @pl.when(pl.program_id(1) == 0)
    def _():
        out_ref[...] = part

    @pl.when(pl.program_id(1) > 0)
    def _():
        out_ref[...] += part


def _peer_ffn(xnt, u, vt, c1, e1, r2, e2):
    d, n = xnt.shape
    n_exp = u.shape[0]
    nk, nh, _ = c1.shape
    bt = min(512, n)
    be = A_PER_BLOCK * nk
    once = pl.Buffered(1)
    return pl.pallas_call(
        _peer_ffn_kernel,
        grid=(n // bt, n_exp // be),
        in_specs=[
            pl.BlockSpec((d, bt), lambda i, j: (0, i), pipeline_mode=once),
            pl.BlockSpec((be, d), lambda i, j: (j, 0)),
            pl.BlockSpec((d, be), lambda i, j: (0, j)),
            pl.BlockSpec((A_PER_BLOCK, nh, bt), lambda i, j: (j, 0, i)),
            pl.BlockSpec((A_PER_BLOCK, nh, bt), lambda i, j: (j, 0, i)),
            pl.BlockSpec((nh, nk, bt), lambda i, j: (0, 0, i), pipeline_mode=once),
            pl.BlockSpec((nh, nk, bt), lambda i, j: (0, 0, i), pipeline_mode=once),
        ],
        out_specs=pl.BlockSpec((d, bt), lambda i, j: (0, i)),
        out_shape=jax.ShapeDtypeStruct((d, n), F32),
        scratch_shapes=[pltpu.VMEM((be, bt), BF16)],
        compiler_params=_params(("parallel", "arbitrary"), 56),
        name="peer_ffn",
    )(xnt, u, vt, c1, e1, r2, e2)


def _add_t_kernel(h_ref, ot_ref, o_ref):
    o_ref[...] = h_ref[...] + ot_ref[...].T


def _add_transposed(h, out_t):
    n, d = h.shape
    bm = min(256, n)
    return pl.pallas_call(
        _add_t_kernel,
        grid=(n // bm,),
        in_specs=[pl.BlockSpec((bm, d), lambda i: (i, 0)), pl.BlockSpec((d, bm), lambda i: (0, i))],
        out_specs=pl.BlockSpec((bm, d), lambda i: (i, 0)),
        out_shape=jax.ShapeDtypeStruct((n, d), F32),
        compiler_params=_params(("parallel",), 56),
        name="add_transposed",
    )(h, out_t)


def _ple_kernel(h_ref, g_ref, wg_ref, p_ref, wp_ref, hcol_ref, o_ref, xn_ref):
    @pl.when(pl.program_id(1) == 0)
    def _():
        def store(rows, xn):
            xn_ref[rows, :] = xn.astype(BF16)

        _rms_row_groups(h_ref, g_ref, store)

    gate = jax.nn.sigmoid(jnp.dot(xn_ref[...], wg_ref[...], preferred_element_type=F32))
    pp = jnp.dot(p_ref[...], wp_ref[...], preferred_element_type=F32)
    o_ref[...] = hcol_ref[...] + pp * gate


def _ple(h, g, w_gate, p, w_proj):
    n, d = h.shape
    bm = min(512, n)
    bn = 512
    pd = p.shape[1]
    return pl.pallas_call(
        _ple_kernel,
        grid=(n // bm, d // bn),
        in_specs=[
            pl.BlockSpec((bm, d), lambda i, j: (i, 0)),
            pl.BlockSpec((1, d), lambda i, j: (0, 0)),
            pl.BlockSpec((d, bn), lambda i, j: (0, j)),
            pl.BlockSpec((bm, pd), lambda i, j: (i, 0)),
            pl.BlockSpec((pd, bn), lambda i, j: (0, j)),
            pl.BlockSpec((bm, bn), lambda i, j: (i, j)),
        ],
        out_specs=pl.BlockSpec((bm, bn), lambda i, j: (i, j)),
        out_shape=jax.ShapeDtypeStruct((n, d), F32),
        scratch_shapes=[pltpu.VMEM((bm, d), BF16)],
        compiler_params=_params(("parallel", "arbitrary"), 56),
        name="ple",
    )(h, g, w_gate, p, w_proj, h)


def _final_norm_kernel(h_ref, g_ref, o_ref):
    def store(rows, xn):
        o_ref[rows, :] = xn

    _rms_row_groups(h_ref, g_ref, store)


def _final_norm(h, g):
    n, d = h.shape
    bm = min(512, n)
    return pl.pallas_call(
        _final_norm_kernel,
        grid=(n // bm,),
        in_specs=[pl.BlockSpec((bm, d), lambda i: (i, 0)), pl.BlockSpec((1, d), lambda i: (0, 0))],
        out_specs=pl.BlockSpec((bm, d), lambda i: (i, 0)),
        out_shape=jax.ShapeDtypeStruct((n, d), F32),
        compiler_params=_params(("parallel",), 48),
        name="final_norm",
    )(h, g)


def _prep_layer(i, g_mix, w_in, gla_w_lr, gla_b_lr, gla_g_out, conf_w_dw, conf_b_dw, conf_ln_g, conf_ln_b,
                sc_w_dw, w_out, g_ffn, peer_w_q, peer_k1, peer_k2, peer_u, peer_v, g_ple, ple_w_gate, ple_w_proj):
    d = w_in.shape[1]
    lr0 = 2 * (d // 4) + 2 * (d // 2)
    w = w_in[i]
    row = lambda x: x[i][None, :]
    return dict(
        g_mix=row(g_mix),
        w_main=jnp.concatenate([w[:, :lr0], w[:, lr0 + GLA_LR:]], axis=1).astype(BF16),
        w_lr_in=jnp.pad(w[:, lr0:lr0 + GLA_LR], ((0, 0), (0, LANES - GLA_LR))).astype(BF16),
        w_lr=jnp.pad(gla_w_lr[i], ((0, LANES - GLA_LR), (0, 0))).astype(BF16),
        b_lr=row(gla_b_lr), gg=row(gla_g_out),
        wc=conf_w_dw[i], bc=row(conf_b_dw), lg=row(conf_ln_g), lb=row(conf_ln_b), ws=sc_w_dw[i],
        w_out=w_out[i].astype(BF16),
        g_ffn=row(g_ffn), wq_t=peer_w_q[i].T.astype(BF16),
        k1=peer_k1[i].astype(BF16), k2=peer_k2[i].astype(BF16),
        u=peer_u[i].astype(BF16), vt=peer_v[i].T.astype(BF16),
        g_ple=row(g_ple), w_gate=ple_w_gate[i].astype(BF16), w_proj=ple_w_proj[i].astype(BF16),
    )


def _layer(h, p, s_gla, s_conf, s_sc, lw, batch, seq):
    proj, la = _in_proj(h, lw["g_mix"], lw["w_main"], lw["w_lr_in"], lw["w_lr"], lw["b_lr"])
    y_gla, new_gla = _gla(proj, la, lw["gg"], s_gla, batch, seq)
    y_conf, y_sc, new_conf, new_sc = _convs(proj, s_conf, s_sc, lw["wc"], lw["bc"], lw["lg"], lw["lb"], lw["ws"],
                                            batch, seq)
    h = _out_proj(y_gla, y_conf, y_sc, lw["w_out"], h)
    xnt, s1, s2 = _scores(h, lw["g_ffn"], lw["wq_t"], lw["k1"], lw["k2"])
    c1, e1, r2, e2 = _topk(s1, s2)
    h = _add_transposed(h, _peer_ffn(xnt, lw["u"], lw["vt"], c1, e1, r2, e2))
    h = _ple(h, lw["g_ple"], lw["w_gate"], p.astype(BF16), lw["w_proj"])
    return h, new_gla, new_conf, new_sc


def _trunk(x, p, s_gla, s_conf, s_sc, layers, g_final):
    batch, seq, d = x.shape
    h = x.reshape(batch * seq, d)
    new_gla, new_conf, new_sc = [], [], []
    for i, lw in enumerate(layers):
        st = (None, None, None) if s_gla is None else (s_gla[i], s_conf[i], s_sc[i])
        h, sg, scf, ssc = _layer(h, p[i].reshape(batch * seq, -1), *st, lw, batch, seq)
        new_gla.append(sg)
        new_conf.append(scf)
        new_sc.append(ssc)
    y = _final_norm(h, g_final[None, :]).reshape(batch, seq, d)
    return y, jnp.stack(new_gla), jnp.stack(new_conf), jnp.stack(new_sc)


def kernel(x_prompt, x_sample, state_gla, state_conf, state_sconv, p_prompt, p_sample, g_mix, w_in, gla_w_lr, gla_b_lr, gla_g_out, conf_w_dw, conf_b_dw, conf_ln_g, conf_ln_b, sc_w_dw, w_out, g_ffn, peer_w_q, peer_k1, peer_k2, peer_u, peer_v, g_ple, ple_w_gate, ple_w_proj, g_final):
    layers = [_prep_layer(i, g_mix, w_in, gla_w_lr, gla_b_lr, gla_g_out, conf_w_dw, conf_b_dw, conf_ln_g,
                          conf_ln_b, sc_w_dw, w_out, g_ffn, peer_w_q, peer_k1, peer_k2, peer_u, peer_v, g_ple,
                          ple_w_gate, ple_w_proj) for i in range(w_in.shape[0])]
    y_p, gla_p, conf_p, sc_p = _trunk(x_prompt, p_prompt, None, None, None, layers, g_final)
    y_s, gla_s, conf_s, sc_s = _trunk(x_sample, p_sample, state_gla, state_conf, state_sconv, layers, g_final)
    return (y_p, y_s, gla_p, conf_p, sc_p, gla_s, conf_s, sc_s)
```

```python
import functools

import jax
import jax.numpy as jnp
from jax import lax
from jax.experimental import pallas as pl
from jax.experimental.pallas import tpu as pltpu

F32 = jnp.float32
BF16 = jnp.bfloat16

EPS = 1e-6
CHUNK = 64
GLA_HEADS = 4
GLA_LR = 16
GLA_TAU = 16.0
CONF_K = 31
SC_K = 3
PEER_HEADS = 8
PEER_NKEYS = 128
PEER_TOPK = 16

LANES = 128
SUBLANES = 8
CONF_HIST = 32
SC_HIST = 8
NORM_ROWS = 128
GLA_ROWS = 512
CONV_ROWS = 256
A_PER_BLOCK = 4
NOT_SELECTED = 64.0
MIB = 1024 * 1024

_CELLS = [(i, j) for i in range(PEER_TOPK) for j in range(PEER_TOPK) if (i + 1) * (j + 1) <= PEER_TOPK]


def _params(semantics, vmem_mib, flags=None):
    return pltpu.CompilerParams(dimension_semantics=semantics, vmem_limit_bytes=vmem_mib * MIB, flags=flags)


def _rms(x, g):
    ms = jnp.mean(x * x, axis=-1, keepdims=True)
    return x * lax.rsqrt(ms + EPS) * g


def _rms_row_groups(h_ref, g_ref, store):
    g = g_ref[...]
    for r0 in range(0, h_ref.shape[0], NORM_ROWS):
        rows = slice(r0, min(r0 + NORM_ROWS, h_ref.shape[0]))
        store(rows, _rms(h_ref[rows, :], g))


def _in_proj_kernel(h_ref, g_ref, w_ref, wlr_in_ref, wlr_ref, blr_ref, proj_ref, la_ref, xn_ref):
    @pl.when(pl.program_id(1) == 0)
    def _():
        def store(rows, xn):
            xn = xn.astype(BF16)
            xn_ref[rows, :] = xn
            lr = jnp.dot(xn, wlr_in_ref[...], preferred_element_type=F32)
            z = jnp.dot(lr.astype(BF16), wlr_ref[...], preferred_element_type=F32) + blr_ref[...]
            la_ref[rows, :] = (jnp.minimum(z, 0.0) - jnp.log1p(jnp.exp(-jnp.abs(z)))) * (1.0 / GLA_TAU)

        _rms_row_groups(h_ref, g_ref, store)

    proj_ref[...] = jnp.dot(xn_ref[...], w_ref[...], preferred_element_type=F32)


def _in_proj(h, g, w_main, w_lr_in, w_lr, b_lr):
    n, d = h.shape
    ncols = w_main.shape[1]
    qk = w_lr.shape[1]
    bm = min(1024, n)
    bn = d // 8
    return pl.pallas_call(
        _in_proj_kernel,
        grid=(n // bm, ncols // bn),
        in_specs=[
            pl.BlockSpec((bm, d), lambda i, j: (i, 0), pipeline_mode=pl.Buffered(1)),
            pl.BlockSpec((1, d), lambda i, j: (0, 0)),
            pl.BlockSpec((d, bn), lambda i, j: (0, j)),
            pl.BlockSpec((d, LANES), lambda i, j: (0, 0)),
            pl.BlockSpec((LANES, qk), lambda i, j: (0, 0)),
            pl.BlockSpec((1, qk), lambda i, j: (0, 0)),
        ],
        out_specs=[
            pl.BlockSpec((bm, bn), lambda i, j: (i, j)),
            pl.BlockSpec((bm, qk), lambda i, j: (i, 0)),
        ],
        out_shape=[jax.ShapeDtypeStruct((n, ncols), F32), jax.ShapeDtypeStruct((n, qk), F32)],
        scratch_shapes=[pltpu.VMEM((bm, d), BF16)],
        compiler_params=_params(("parallel", "arbitrary"), 56),
        name="in_proj",
    )(h, g, w_main, w_lr_in, w_lr, b_lr)


def _gla_kernel(*refs, chunk, n_chunks, has_state, scale):
    if has_state:
        q_ref, k_ref, v_ref, g_ref, la_ref, gg_ref, s0_ref, y_ref, sout_ref, st_ref = refs
    else:
        q_ref, k_ref, v_ref, g_ref, la_ref, gg_ref, y_ref, sout_ref, st_ref = refs
    tt = pl.program_id(2)

    @pl.when(tt == 0)
    def _():
        if has_state:
            st_ref[...] = s0_ref[0, 0].T
        else:
            st_ref[...] = jnp.zeros_like(st_ref)

    row = lax.broadcasted_iota(jnp.int32, (chunk, chunk), 0)
    col = lax.broadcasted_iota(jnp.int32, (chunk, chunk), 1)
    causal = row >= col
    tril = jnp.where(causal, 1.0, 0.0).astype(BF16)
    mid = (chunk - 1) // 2
    nt = (((1,), (1,)), ((), ()))
    tn = (((0,), (0,)), ((), ()))

    for c in range(n_chunks):
        sl = pl.ds(c * chunk, chunk)
        la = la_ref[sl, :]
        la_hi = la.astype(BF16)
        r1 = la - la_hi.astype(F32)
        la_mid = r1.astype(BF16)
        la_lo = (r1 - la_mid.astype(F32)).astype(BF16)
        cb = (jnp.dot(tril, la_hi, preferred_element_type=F32)
              + jnp.dot(tril, la_mid, preferred_element_type=F32)
              + jnp.dot(tril, la_lo, preferred_element_type=F32))
        c_last = cb[chunk - 1:chunk, :]
        c_mid = cb[mid:mid + 1, :]
        q = q_ref[sl, :] * scale
        k = k_ref[sl, :]
        vb = v_ref[sl, :].astype(BF16)
        st = st_ref[...]
        qe = (q * jnp.exp(cb - c_mid)).astype(BF16)
        ke = (k * jnp.exp(c_mid - cb)).astype(BF16)
        att = lax.dot_general(qe, ke, nt, preferred_element_type=F32)
        att = jnp.where(causal, att, 0.0).astype(BF16)
        q2 = (q * jnp.exp(cb)).astype(BF16)
        o = (jnp.dot(att, vb, preferred_element_type=F32)
             + lax.dot_general(q2, st.astype(BF16), nt, preferred_element_type=F32))
        k2 = (k * jnp.exp(c_last - cb)).astype(BF16)
        st_ref[...] = jnp.exp(c_last) * st + lax.dot_general(vb, k2, tn, preferred_element_type=F32)
        o = o * lax.rsqrt(jnp.mean(o * o, axis=-1, keepdims=True) + EPS)
        gate = g_ref[sl, :]
        y_ref[sl, :] = (o * gg_ref[...] * (gate * jax.nn.sigmoid(gate))).astype(y_ref.dtype)

    @pl.when(tt == pl.num_programs(2) - 1)
    def _():
        sout_ref[0, 0] = st_ref[...].T


def _gla(proj, la, gg, s0, batch, seq):
    n = proj.shape[0]
    qk = la.shape[1]
    v_w = gg.shape[1]
    dk = qk // GLA_HEADS
    dv = v_w // GLA_HEADS
    chunk = min(CHUNK, seq)
    tb = min(seq, GLA_ROWS)
    assert seq % tb == 0 and tb % chunk == 0
    nt = seq // tb
    has_state = s0 is not None
    k_blk = qk // dk
    v_blk = 2 * qk // dv
    g_blk = (2 * qk + v_w) // dv
    rows = lambda b, h, t: b * nt + t
    in_specs = [
        pl.BlockSpec((tb, dk), lambda b, h, t: (rows(b, h, t), h)),
        pl.BlockSpec((tb, dk), lambda b, h, t: (rows(b, h, t), k_blk + h)),
        pl.BlockSpec((tb, dv), lambda b, h, t: (rows(b, h, t), v_blk + h)),
        pl.BlockSpec((tb, dv), lambda b, h, t: (rows(b, h, t), g_blk + h)),
        pl.BlockSpec((tb, dk), lambda b, h, t: (rows(b, h, t), h)),
        pl.BlockSpec((1, dv), lambda b, h, t: (0, h)),
    ]
    args = [proj, proj, proj, proj, la, gg]
    if has_state:
        in_specs.append(pl.BlockSpec((1, 1, dk, dv), lambda b, h, t: (b, h, 0, 0)))
        args.append(s0)
    return pl.pallas_call(
        functools.partial(_gla_kernel, chunk=chunk, n_chunks=tb // chunk, has_state=has_state, scale=dk ** -0.5),
        grid=(batch, GLA_HEADS, nt),
        in_specs=in_specs,
        out_specs=[
            pl.BlockSpec((tb, dv), lambda b, h, t: (rows(b, h, t), h)),
            pl.BlockSpec((1, 1, dk, dv), lambda b, h, t: (b, h, 0, 0)),
        ],
        out_shape=[jax.ShapeDtypeStruct((n, v_w), BF16),
                   jax.ShapeDtypeStruct((batch, GLA_HEADS, dk, dv), F32)],
        scratch_shapes=[pltpu.VMEM((dv, dk), F32)],
        compiler_params=_params(("parallel", "parallel", "arbitrary"), 32),
        name="gla",
    )(*args)


def _conv_kernel(*refs, tb, has_state):
    if has_state:
        (ca_ref, cg_ref, sb_ref, scg_ref, sh_ref, wc_ref, bc_ref, lg_ref, lb_ref, ws_ref, hc_ref, hs_ref,
         yc_ref, ys_ref, nc_ref, ns_ref, ubuf, zbuf) = refs
    else:
        (ca_ref, cg_ref, sb_ref, scg_ref, sh_ref, wc_ref, bc_ref, lg_ref, lb_ref, ws_ref,
         yc_ref, ys_ref, nc_ref, ns_ref, ubuf, zbuf) = refs
    tt = pl.program_id(1)
    c_skip = CONF_HIST - (CONF_K - 1)
    s_skip = SC_HIST - (SC_K - 1)

    @pl.when(tt == 0)
    def _():
        ubuf[0:CONF_HIST, :] = jnp.zeros((CONF_HIST, ubuf.shape[1]), F32)
        zbuf[0:SC_HIST, :] = jnp.zeros((SC_HIST, zbuf.shape[1]), F32)
        if has_state:
            ubuf[c_skip:CONF_HIST, :] = hc_ref[0]
            zbuf[s_skip:SC_HIST, :] = hs_ref[0]

    @pl.when(tt > 0)
    def _():
        ubuf[0:CONF_HIST, :] = ubuf[tb:tb + CONF_HIST, :]
        zbuf[0:SC_HIST, :] = zbuf[tb:tb + SC_HIST, :]

    ubuf[CONF_HIST:CONF_HIST + tb, :] = ca_ref[...] * jax.nn.sigmoid(cg_ref[...])
    zbuf[SC_HIST:SC_HIST + tb, :] = scg_ref[...] * sh_ref[...]

    acc = wc_ref[0:1, :] * ubuf[c_skip:c_skip + tb, :]
    for j in range(1, CONF_K):
        acc = acc + wc_ref[j:j + 1, :] * ubuf[c_skip + j:c_skip + j + tb, :]
    c = acc + bc_ref[...]
    mu = jnp.mean(c, axis=-1, keepdims=True)
    cc = c - mu
    var = jnp.mean(cc * cc, axis=-1, keepdims=True)
    y = cc * lax.rsqrt(var + EPS) * lg_ref[...] + lb_ref[...]
    yc_ref[...] = (y * jax.nn.sigmoid(y)).astype(yc_ref.dtype)

    z = ws_ref[0:1, :] * zbuf[s_skip:s_skip + tb, :]
    for j in range(1, SC_K):
        z = z + ws_ref[j:j + 1, :] * zbuf[s_skip + j:s_skip + j + tb, :]
    ys_ref[...] = (sb_ref[...] * z).astype(ys_ref.dtype)

    @pl.when(tt == pl.num_programs(1) - 1)
    def _():
        nc_ref[0] = ubuf[tb + c_skip:tb + CONF_HIST, :]
        ns_ref[0] = zbuf[tb + s_skip:tb + SC_HIST, :]


def _convs(proj, hc, hs, wc, bc, lg, lb, ws, batch, seq):
    n = proj.shape[0]
    w = wc.shape[1]
    tb = min(seq, CONV_ROWS)
    nt = seq // tb
    assert seq % tb == 0 and (nt == 1 or tb >= CONF_HIST)
    has_state = hc is not None
    first = (proj.shape[1] - 5 * w) // w
    rows = lambda b, t: b * nt + t
    col = lambda c: pl.BlockSpec((tb, w), lambda b, t: (rows(b, t), first + c))
    vec = lambda r: pl.BlockSpec((r, w), lambda b, t: (0, 0))
    in_specs = [col(0), col(1), col(2), col(3), col(4), vec(CONF_K), vec(1), vec(1), vec(1), vec(SC_K)]
    args = [proj] * 5 + [wc, bc, lg, lb, ws]
    if has_state:
        in_specs += [pl.BlockSpec((1, CONF_K - 1, w), lambda b, t: (b, 0, 0)),
                     pl.BlockSpec((1, SC_K - 1, w), lambda b, t: (b, 0, 0))]
        args += [hc, hs]
    return pl.pallas_call(
        functools.partial(_conv_kernel, tb=tb, has_state=has_state),
        grid=(batch, nt),
        in_specs=in_specs,
        out_specs=[
            pl.BlockSpec((tb, w), lambda b, t: (rows(b, t), 0)),
            pl.BlockSpec((tb, w), lambda b, t: (rows(b, t), 0)),
            pl.BlockSpec((1, CONF_K - 1, w), lambda b, t: (b, 0, 0)),
            pl.BlockSpec((1, SC_K - 1, w), lambda b, t: (b, 0, 0)),
        ],
        out_shape=[jax.ShapeDtypeStruct((n, w), BF16), jax.ShapeDtypeStruct((n, w), BF16),
                   jax.ShapeDtypeStruct((batch, CONF_K - 1, w), F32),
                   jax.ShapeDtypeStruct((batch, SC_K - 1, w), F32)],
        scratch_shapes=[pltpu.VMEM((CONF_HIST + tb, w), F32), pltpu.VMEM((SC_HIST + tb, w), F32)],
        compiler_params=_params(("parallel", "arbitrary"), 32),
        name="convs",
    )(*args)


def _out_proj_kernel(yg_ref, yc_ref, ys_ref, w_ref, h_ref, o_ref):
    v_w = yg_ref.shape[1]
    c_w = yc_ref.shape[1]
    acc = jnp.dot(yg_ref[...], w_ref[0:v_w, :], preferred_element_type=F32)
    acc = acc + jnp.dot(yc_ref[...], w_ref[v_w:v_w + c_w, :], preferred_element_type=F32)
    acc = acc + jnp.dot(ys_ref[...], w_ref[v_w + c_w:, :], preferred_element_type=F32)
    o_ref[...] = h_ref[...] + acc


def _out_proj(yg, yc, ys, w_out, h):
    n, d = h.shape
    bm = min(1024, n)
    bn = 512
    return pl.pallas_call(
        _out_proj_kernel,
        grid=(n // bm, d // bn),
        in_specs=[
            pl.BlockSpec((bm, yg.shape[1]), lambda i, j: (i, 0)),
            pl.BlockSpec((bm, yc.shape[1]), lambda i, j: (i, 0)),
            pl.BlockSpec((bm, ys.shape[1]), lambda i, j: (i, 0)),
            pl.BlockSpec((w_out.shape[0], bn), lambda i, j: (0, j)),
            pl.BlockSpec((bm, bn), lambda i, j: (i, j)),
        ],
        out_specs=pl.BlockSpec((bm, bn), lambda i, j: (i, j)),
        out_shape=jax.ShapeDtypeStruct((n, d), F32),
        compiler_params=_params(("parallel", "arbitrary"), 48),
        name="out_proj",
    )(yg, yc, ys, w_out, h)


def _scores_kernel(h_ref, g_ref, wq_ref, k1_ref, k2_ref, xnt_ref, s1_ref, s2_ref):
    @pl.when(pl.program_id(1) == 0)
    def _():
        def store(rows, xn):
            xnt_ref[:, rows] = xn.T.astype(BF16)

        _rms_row_groups(h_ref, g_ref, store)

    q = jnp.dot(wq_ref[...], xnt_ref[...], preferred_element_type=F32)
    half = q.shape[0] // 2
    s1_ref[0] = jnp.dot(k1_ref[0], q[:half].astype(BF16), preferred_element_type=F32)
    s2_ref[0] = jnp.dot(k2_ref[0], q[half:].astype(BF16), preferred_element_type=F32)


def _scores(h, g, wq_t, k1, k2):
    n, d = h.shape
    bt = min(512, n)
    dq = wq_t.shape[0] // PEER_HEADS
    nk = k1.shape[1]
    return pl.pallas_call(
        _scores_kernel,
        grid=(n // bt, PEER_HEADS),
        in_specs=[
            pl.BlockSpec((bt, d), lambda i, hd: (i, 0)),
            pl.BlockSpec((1, d), lambda i, hd: (0, 0)),
            pl.BlockSpec((dq, d), lambda i, hd: (hd, 0)),
            pl.BlockSpec((1, nk, dq // 2), lambda i, hd: (hd, 0, 0)),
            pl.BlockSpec((1, nk, dq // 2), lambda i, hd: (hd, 0, 0)),
        ],
        out_specs=[
            pl.BlockSpec((d, bt), lambda i, hd: (0, i)),
            pl.BlockSpec((1, nk, bt), lambda i, hd: (hd, 0, i)),
            pl.BlockSpec((1, nk, bt), lambda i, hd: (hd, 0, i)),
        ],
        out_shape=[jax.ShapeDtypeStruct((d, n), BF16),
                   jax.ShapeDtypeStruct((PEER_HEADS, nk, n), F32),
                   jax.ShapeDtypeStruct((PEER_HEADS, nk, n), F32)],
        compiler_params=_params(("parallel", "arbitrary"), 48),
        name="peer_scores",
    )(h, g, wq_t, k1, k2)


def _oddeven_merge_sort_pairs(n):
    pairs = []
    p = 1
    while p < n:
        k = p
        while k >= 1:
            for j in range(k % p, n - k, 2 * k):
                for i in range(min(k, n - j - k)):
                    if (i + j) // (2 * p) == (i + j + k) // (2 * p):
                        pairs.append((i + j, i + j + k))
            k //= 2
        p *= 2
    return pairs


def _bitonic_merge_pairs(n):
    pairs = []
    k = n // 2
    while k >= 1:
        pairs += [(i, i + k) for i in range(n) if (i // k) % 2 == 0]
        k //= 2
    return pairs


_SORT_PAIRS = _oddeven_merge_sort_pairs(PEER_TOPK)
_MERGE_PAIRS = _bitonic_merge_pairs(PEER_TOPK)


def _exchange(v, pairs):
    v = list(v)
    for i, j in pairs:
        v[i], v[j] = jnp.maximum(v[i], v[j]), jnp.minimum(v[i], v[j])
    return v


def _key_tile(a):
    return pl.ds(pl.multiple_of(a * PEER_HEADS, PEER_HEADS), PEER_HEADS)


def _top_values(s_ref, s_scr, g_scr, v_scr):
    nh, nk, k = PEER_HEADS, PEER_NKEYS, PEER_TOPK
    n_groups = nk // k
    for grp in range(n_groups):
        vals = []
        for i in range(k):
            a = grp * k + i
            t = s_ref[pl.ds(a, nh, stride=nk), :]
            s_scr[a * nh:(a + 1) * nh, :] = t
            vals.append(t)
        for i, t in enumerate(_exchange(vals, _SORT_PAIRS)):
            g_scr[(grp * k + i) * nh:(grp * k + i + 1) * nh, :] = t
    span = 1
    while span < n_groups:
        for grp in range(0, n_groups, 2 * span):
            lo = [g_scr[(grp * k + i) * nh:(grp * k + i + 1) * nh, :] for i in range(k)]
            hi = [g_scr[((grp + span) * k + i) * nh:((grp + span) * k + i + 1) * nh, :] for i in range(k)]
            top = _exchange([jnp.maximum(lo[i], hi[k - 1 - i]) for i in range(k)], _MERGE_PAIRS)
            for i, t in enumerate(top):
                g_scr[(grp * k + i) * nh:(grp * k + i + 1) * nh, :] = t
        span *= 2
    for i in range(k):
        v_scr[i] = g_scr[i * nh:(i + 1) * nh, :]


def _ties(s_scr, v_scr):
    k = PEER_TOPK
    v = [v_scr[i] for i in range(k)]
    bad = jnp.zeros_like(v[0])
    for i in range(k - 1):
        bad = bad + jnp.where(v[i] == v[i + 1], 1.0, 0.0)

    def count(a, c):
        return c + jnp.where(s_scr[_key_tile(a), :] >= v[k - 1], 1.0, 0.0)

    n_ge = lax.fori_loop(0, PEER_NKEYS, count, jnp.zeros_like(v[0]), unroll=8)
    return bad + jnp.where(n_ge == float(k), 0.0, 1.0)


def _rank_keys(s_scr, r_scr):
    nh, nk = PEER_HEADS, PEER_NKEYS
    bt = s_scr.shape[1]

    def init(a, m):
        r_scr[_key_tile(a), :] = jnp.full((nh, bt), NOT_SELECTED, F32)
        return jnp.maximum(m, s_scr[_key_tile(a), :])

    m0 = lax.fori_loop(0, nk, init, jnp.full((nh, bt), -jnp.inf, F32), unroll=8)

    def round_body(r, m):
        def find(a, idx):
            return jnp.minimum(idx, jnp.where(s_scr[_key_tile(a), :] == m, a.astype(F32), float(nk)))

        idx = lax.fori_loop(0, nk, find, jnp.full((nh, bt), float(nk), F32), unroll=8)
        rank = r.astype(F32)

        def knock(a, m_next):
            hit = idx == a.astype(F32)
            s_new = jnp.where(hit, -jnp.inf, s_scr[_key_tile(a), :])
            s_scr[_key_tile(a), :] = s_new
            r_scr[_key_tile(a), :] = jnp.where(hit, rank, r_scr[_key_tile(a), :])
            return jnp.maximum(m_next, s_new)

        return lax.fori_loop(0, nk, knock, jnp.full((nh, bt), -jnp.inf, F32), unroll=8)

    lax.fori_loop(0, PEER_TOPK, round_body, m0)


def _topk_kernel(s1_ref, s2_ref, c1_ref, e1_ref, r2_ref, e2_ref,
                 s1_scr, s2_scr, g_scr, r2_scr, e2k_scr, v1_scr, v2_scr, c_scr, e1z_scr, e2_scr, t_scr):
    nh, nk, k = PEER_HEADS, PEER_NKEYS, PEER_TOPK
    _top_values(s1_ref, s1_scr, g_scr, v1_scr)
    _top_values(s2_ref, s2_scr, g_scr, v2_scr)

    v1 = [v1_scr[i] for i in range(PEER_TOPK)]
    v2 = [v2_scr[j] for j in range(PEER_TOPK)]
    cand = {(i, j): v1[i] + v2[j] for (i, j) in _CELLS}
    before = {p: 0.0 for p in _CELLS}
    for x, p in enumerate(_CELLS):
        for q in _CELLS[x + 1:]:
            if p[0] <= q[0] and p[1] <= q[1]:
                before[q] = before[q] + 1.0
            elif not (q[0] <= p[0] and q[1] <= p[1]):
                p_first = jnp.where(cand[p] >= cand[q], 1.0, 0.0)
                before[q] = before[q] + p_first
                before[p] = before[p] + (1.0 - p_first)
    e1 = [jnp.exp(v1[i] - v1[0]) for i in range(PEER_TOPK)]
    e2 = [jnp.exp(v2[j] - v2[0]) for j in range(PEER_TOPK)]
    height = [jnp.zeros_like(v1[0]) for _ in range(PEER_TOPK)]
    z = jnp.zeros_like(v1[0])
    for (i, j) in _CELLS:
        sel = jnp.where(before[(i, j)] < float(PEER_TOPK), 1.0, 0.0)
        height[i] = height[i] + sel
        z = z + sel * (e1[i] * e2[j])
    inv_z = 1.0 / z
    for i in range(PEER_TOPK):
        c_scr[i] = height[i]
        e1z_scr[i] = e1[i] * inv_z
        e2_scr[i] = e2[i]
    for c in range(1, k + 1):
        t = jnp.full_like(v1[0], jnp.inf)
        for i in range(k):
            t = jnp.minimum(t, jnp.where(height[i] >= float(c), v1[i], jnp.inf))
        t_scr[c - 1] = t

    n_ties = jnp.sum(_ties(s1_scr, v1_scr) + _ties(s2_scr, v2_scr))

    @pl.when(n_ties == 0.0)
    def _():
        v1_max = v1_scr[0]
        v2_max = v2_scr[0]
        inv = e1z_scr[0]

        def emit1(a, carry):
            s = s1_scr[_key_tile(a), :]
            c = jnp.zeros_like(s)
            for lvl in range(k):
                c = c + jnp.where(s >= t_scr[lvl], 1.0, 0.0)
            c1_ref[a] = c
            e1_ref[a] = jnp.exp(s - v1_max) * inv
            return carry

        lax.fori_loop(0, nk, emit1, 0, unroll=4)

        def emit2(b, carry):
            s = s2_scr[_key_tile(b), :]
            r = jnp.zeros_like(s)
            for i in range(k):
                r = r + jnp.where(v2_scr[i] > s, 1.0, 0.0)
            r2_scr[_key_tile(b), :] = r
            e2k_scr[_key_tile(b), :] = jnp.exp(s - v2_max)
            return carry

        lax.fori_loop(0, nk, emit2, 0, unroll=4)

    @pl.when(n_ties != 0.0)
    def _():
        _rank_keys(s1_scr, g_scr)

        def emit1(a, carry):
            rk = g_scr[_key_tile(a), :]
            c = jnp.zeros_like(rk)
            e = jnp.zeros_like(rk)
            for r in range(k):
                eq = rk == float(r)
                c = jnp.where(eq, c_scr[r], c)
                e = jnp.where(eq, e1z_scr[r], e)
            c1_ref[a] = c
            e1_ref[a] = e
            return carry

        lax.fori_loop(0, nk, emit1, 0, unroll=4)
        _rank_keys(s2_scr, r2_scr)

        def emit2(b, carry):
            rk = r2_scr[_key_tile(b), :]
            e = jnp.zeros_like(rk)
            for r in range(k):
                e = jnp.where(rk == float(r), e2_scr[r], e)
            e2k_scr[_key_tile(b), :] = e
            return carry

        lax.fori_loop(0, nk, emit2, 0, unroll=4)

    for hd in range(nh):
        r2_ref[hd] = r2_scr[pl.ds(hd, nk, stride=nh), :]
        e2_ref[hd] = e2k_scr[pl.ds(hd, nk, stride=nh), :]


def _topk(s1, s2):
    nh, nk, n = s1.shape
    bt = LANES
    rows = nh * nk
    big = pltpu.VMEM((rows, bt), F32)
    small = pltpu.VMEM((PEER_TOPK, nh, bt), F32)
    return pl.pallas_call(
        _topk_kernel,
        grid=(n // bt,),
        in_specs=[pl.BlockSpec((rows, bt), lambda i: (0, i)), pl.BlockSpec((rows, bt), lambda i: (0, i))],
        out_specs=[
            pl.BlockSpec((nk, nh, bt), lambda i: (0, 0, i)),
            pl.BlockSpec((nk, nh, bt), lambda i: (0, 0, i)),
            pl.BlockSpec((nh, nk, bt), lambda i: (0, 0, i)),
            pl.BlockSpec((nh, nk, bt), lambda i: (0, 0, i)),
        ],
        out_shape=[jax.ShapeDtypeStruct((nk, nh, n), F32), jax.ShapeDtypeStruct((nk, nh, n), F32),
                   jax.ShapeDtypeStruct((nh, nk, n), F32), jax.ShapeDtypeStruct((nh, nk, n), F32)],
        scratch_shapes=[big] * 5 + [small] * 6,
        compiler_params=_params(("parallel",), 40),
        name="peer_topk",
    )(s1.reshape(rows, n), s2.reshape(rows, n))


def _peer_ffn_kernel(xnt_ref, u_ref, v_ref, c1_ref, e1_ref, r2_ref, e2_ref, out_ref, w_scr):
    nk = PEER_NKEYS
    j = pl.program_id(1)
    bt = out_ref.shape[1]

    @pl.when(j == 0)
    def _():
        out_ref[...] = jnp.zeros_like(out_ref)
        w_scr[1] = jnp.zeros(w_scr.shape[1:], BF16)

    out_ref[...] += lax.dot_general(v_ref[...], w_scr[(j + 1) % 2], (((0,), (0,)), ((), ())),
                                    preferred_element_type=F32)

    at = jnp.dot(u_ref[...], xnt_ref[...], preferred_element_type=F32)
    cur = j % 2
    for al in range(A_PER_BLOCK):
        rows = slice(al * nk, (al + 1) * nk)
        for l0 in range(0, bt, LANES):
            lanes = slice(l0, l0 + LANES)
            g = jnp.zeros((nk, LANES), F32)
            for hd in range(PEER_HEADS):
                sel = r2_ref[hd, :, lanes] < c1_ref[al, hd:hd + 1, lanes]
                g = g + jnp.where(sel, e1_ref[al, hd:hd + 1, lanes] * e2_ref[hd, :, lanes], 0.0)
            w_scr[cur, rows, lanes] = (g * jax.nn.gelu(at[rows, lanes])).astype(BF16)


def _peer_ffn(xnt, u, v, c1, e1, r2, e2):
    d, n = xnt.shape
    n_exp = u.shape[0]
    nk, nh, _ = c1.shape
    bt = min(512, n)
    be = A_PER_BLOCK * nk
    nj = n_exp // be
    once = pl.Buffered(1)
    return pl.pallas_call(
        _peer_ffn_kernel,
        grid=(n // bt, nj + 1),
        in_specs=[
            pl.BlockSpec((d, bt), lambda i, j: (0, i), pipeline_mode=once),
            pl.BlockSpec((be, d), lambda i, j: (jnp.minimum(j, nj - 1), 0)),
            pl.BlockSpec((be, d), lambda i, j: (jnp.maximum(j - 1, 0), 0)),
            pl.BlockSpec((A_PER_BLOCK, nh, bt), lambda i, j: (jnp.minimum(j, nj - 1), 0, i)),
            pl.BlockSpec((A_PER_BLOCK, nh, bt), lambda i, j: (jnp.minimum(j, nj - 1), 0, i)),
            pl.BlockSpec((nh, nk, bt), lambda i, j: (0, 0, i), pipeline_mode=once),
            pl.BlockSpec((nh, nk, bt), lambda i, j: (0, 0, i), pipeline_mode=once),
        ],
        out_specs=pl.BlockSpec((d, bt), lambda i, j: (0, i)),
        out_shape=jax.ShapeDtypeStruct((d, n), F32),
        scratch_shapes=[pltpu.VMEM((2, be, bt), BF16)],
        compiler_params=_params(("parallel", "arbitrary"), 56),
        name="peer_ffn",
    )(xnt, u, v, c1, e1, r2, e2)


def _add_t_kernel(h_ref, ot_ref, o_ref):
    o_ref[...] = h_ref[...] + ot_ref[...].T


def _add_transposed(h, out_t):
    n, d = h.shape
    bm = min(256, n)
    return pl.pallas_call(
        _add_t_kernel,
        grid=(n // bm,),
        in_specs=[pl.BlockSpec((bm, d), lambda i: (i, 0)), pl.BlockSpec((d, bm), lambda i: (0, i))],
        out_specs=pl.BlockSpec((bm, d), lambda i: (i, 0)),
        out_shape=jax.ShapeDtypeStruct((n, d), F32),
        compiler_params=_params(("parallel",), 56),
        name="add_transposed",
    )(h, out_t)


def _ple_kernel(h_ref, g_ref, wg_ref, p_ref, wp_ref, hcol_ref, o_ref, xn_ref):
    @pl.when(pl.program_id(1) == 0)
    def _():
        def store(rows, xn):
            xn_ref[rows, :] = xn.astype(BF16)

        _rms_row_groups(h_ref, g_ref, store)

    gate = jax.nn.sigmoid(jnp.dot(xn_ref[...], wg_ref[...], preferred_element_type=F32))
    pp = jnp.dot(p_ref[...], wp_ref[...], preferred_element_type=F32)
    o_ref[...] = hcol_ref[...] + pp * gate


def _ple(h, g, w_gate, p, w_proj):
    n, d = h.shape
    bm = min(1024, n)
    bn = 512
    pd = p.shape[1]
    return pl.pallas_call(
        _ple_kernel,
        grid=(n // bm, d // bn),
        in_specs=[
            pl.BlockSpec((bm, d), lambda i, j: (i, 0), pipeline_mode=pl.Buffered(1)),
            pl.BlockSpec((1, d), lambda i, j: (0, 0)),
            pl.BlockSpec((d, bn), lambda i, j: (0, j)),
            pl.BlockSpec((bm, pd), lambda i, j: (i, 0)),
            pl.BlockSpec((pd, bn), lambda i, j: (0, j)),
            pl.BlockSpec((bm, bn), lambda i, j: (i, j)),
        ],
        out_specs=pl.BlockSpec((bm, bn), lambda i, j: (i, j)),
        out_shape=jax.ShapeDtypeStruct((n, d), F32),
        scratch_shapes=[pltpu.VMEM((bm, d), BF16)],
        compiler_params=_params(("parallel", "arbitrary"), 56),
        name="ple",
    )(h, g, w_gate, p, w_proj, h)


def _final_norm_kernel(h_ref, g_ref, o_ref):
    def store(rows, xn):
        o_ref[rows, :] = xn

    _rms_row_groups(h_ref, g_ref, store)


def _final_norm(h, g):
    n, d = h.shape
    bm = min(512, n)
    return pl.pallas_call(
        _final_norm_kernel,
        grid=(n // bm,),
        in_specs=[pl.BlockSpec((bm, d), lambda i: (i, 0)), pl.BlockSpec((1, d), lambda i: (0, 0))],
        out_specs=pl.BlockSpec((bm, d), lambda i: (i, 0)),
        out_shape=jax.ShapeDtypeStruct((n, d), F32),
        compiler_params=_params(("parallel",), 48),
        name="final_norm",
    )(h, g)


def _prep_layer(i, g_mix, w_in, gla_w_lr, gla_b_lr, gla_g_out, conf_w_dw, conf_b_dw, conf_ln_g, conf_ln_b,
                sc_w_dw, w_out, g_ffn, peer_w_q, peer_k1, peer_k2, peer_u, peer_v, g_ple, ple_w_gate, ple_w_proj):
    d = w_in.shape[1]
    lr0 = 2 * (d // 4) + 2 * (d // 2)
    w = w_in[i]
    row = lambda x: x[i][None, :]
    return dict(
        g_mix=row(g_mix),
        w_main=jnp.concatenate([w[:, :lr0], w[:, lr0 + GLA_LR:]], axis=1).astype(BF16),
        w_lr_in=jnp.pad(w[:, lr0:lr0 + GLA_LR], ((0, 0), (0, LANES - GLA_LR))).astype(BF16),
        w_lr=jnp.pad(gla_w_lr[i], ((0, LANES - GLA_LR), (0, 0))).astype(BF16),
        b_lr=row(gla_b_lr), gg=row(gla_g_out),
        wc=conf_w_dw[i], bc=row(conf_b_dw), lg=row(conf_ln_g), lb=row(conf_ln_b), ws=sc_w_dw[i],
        w_out=w_out[i].astype(BF16),
        g_ffn=row(g_ffn), wq_t=peer_w_q[i].T.astype(BF16),
        k1=peer_k1[i].astype(BF16), k2=peer_k2[i].astype(BF16),
        u=peer_u[i].astype(BF16), v=peer_v[i].astype(BF16),
        g_ple=row(g_ple), w_gate=ple_w_gate[i].astype(BF16), w_proj=ple_w_proj[i].astype(BF16),
    )


def _layer(h, p, s_gla, s_conf, s_sc, lw, batch, seq):
    proj, la = _in_proj(h, lw["g_mix"], lw["w_main"], lw["w_lr_in"], lw["w_lr"], lw["b_lr"])
    y_gla, new_gla = _gla(proj, la, lw["gg"], s_gla, batch, seq)
    y_conf, y_sc, new_conf, new_sc = _convs(proj, s_conf, s_sc, lw["wc"], lw["bc"], lw["lg"], lw["lb"], lw["ws"],
                                            batch, seq)
    h = _out_proj(y_gla, y_conf, y_sc, lw["w_out"], h)
    xnt, s1, s2 = _scores(h, lw["g_ffn"], lw["wq_t"], lw["k1"], lw["k2"])
    c1, e1, r2, e2 = _topk(s1, s2)
    h = _add_transposed(h, _peer_ffn(xnt, lw["u"], lw["v"], c1, e1, r2, e2))
    h = _ple(h, lw["g_ple"], lw["w_gate"], p.astype(BF16), lw["w_proj"])
    return h, new_gla, new_conf, new_sc


def _trunk(x, p, s_gla, s_conf, s_sc, layers, g_final):
    batch, seq, d = x.shape
    h = x.reshape(batch * seq, d)
    new_gla, new_conf, new_sc = [], [], []
    for i, lw in enumerate(layers):
        st = (None, None, None) if s_gla is None else (s_gla[i], s_conf[i], s_sc[i])
        h, sg, scf, ssc = _layer(h, p[i].reshape(batch * seq, -1), *st, lw, batch, seq)
        new_gla.append(sg)
        new_conf.append(scf)
        new_sc.append(ssc)
    y = _final_norm(h, g_final[None, :]).reshape(batch, seq, d)
    return y, jnp.stack(new_gla), jnp.stack(new_conf), jnp.stack(new_sc)


def kernel(x_prompt, x_sample, state_gla, state_conf, state_sconv, p_prompt, p_sample, g_mix, w_in, gla_w_lr, gla_b_lr, gla_g_out, conf_w_dw, conf_b_dw, conf_ln_g, conf_ln_b, sc_w_dw, w_out, g_ffn, peer_w_q, peer_k1, peer_k2, peer_u, peer_v, g_ple, ple_w_gate, ple_w_proj, g_final):
    layers = [_prep_layer(i, g_mix, w_in, gla_w_lr, gla_b_lr, gla_g_out, conf_w_dw, conf_b_dw, conf_ln_g,
                          conf_ln_b, sc_w_dw, w_out, g_ffn, peer_w_q, peer_k1, peer_k2, peer_u, peer_v, g_ple,
                          ple_w_gate, ple_w_proj) for i in range(w_in.shape[0])]
    y_p, gla_p, conf_p, sc_p = _trunk(x_prompt, p_prompt, None, None, None, layers, g_final)
    y_s, gla_s, conf_s, sc_s = _trunk(x_sample, p_sample, state_gla, state_conf, state_sconv, layers, g_final)
    return (y_p, y_s, gla_p, conf_p, sc_p, gla_s, conf_s, sc_s)
```

```python
import functools

import jax
import jax.numpy as jnp
from jax import lax
from jax.experimental import pallas as pl
from jax.experimental.pallas import tpu as pltpu

F32 = jnp.float32
BF16 = jnp.bfloat16

EPS = 1e-6
CHUNK = 64
GLA_HEADS = 4
GLA_LR = 16
GLA_TAU = 16.0
CONF_K = 31
SC_K = 3
PEER_HEADS = 8
PEER_NKEYS = 128
PEER_TOPK = 16

LANES = 128
SUBLANES = 8
CONF_HIST = 32
SC_HIST = 8
NORM_ROWS = 128
CAST_ROWS = 512
GLA_ROWS = 512
CONV_ROWS = 256
A_PER_BLOCK = 4
NOT_SELECTED = 64.0
MIB = 1024 * 1024

_CELLS = [(i, j) for i in range(PEER_TOPK) for j in range(PEER_TOPK) if (i + 1) * (j + 1) <= PEER_TOPK]


def _params(semantics, vmem_mib, flags=None):
    return pltpu.CompilerParams(dimension_semantics=semantics, vmem_limit_bytes=vmem_mib * MIB, flags=flags)


def _rms(x, g):
    ms = jnp.mean(x * x, axis=-1, keepdims=True)
    return x * lax.rsqrt(ms + EPS) * g


def _rms_row_groups(h_ref, g_ref, store):
    g = g_ref[...]
    for r0 in range(0, h_ref.shape[0], NORM_ROWS):
        rows = slice(r0, min(r0 + NORM_ROWS, h_ref.shape[0]))
        store(rows, _rms(h_ref[rows, :], g))


def _in_proj_kernel(h_ref, g_ref, w_ref, wlr_in_ref, wlr_ref, blr_ref, proj_ref, la_ref, xn_ref):
    @pl.when(pl.program_id(1) == 0)
    def _():
        def store(rows, xn):
            xn = xn.astype(BF16)
            xn_ref[rows, :] = xn
            lr = jnp.dot(xn, wlr_in_ref[...], preferred_element_type=F32)
            z = jnp.dot(lr.astype(BF16), wlr_ref[...], preferred_element_type=F32) + blr_ref[...]
            la_ref[rows, :] = (jnp.minimum(z, 0.0) - jnp.log1p(jnp.exp(-jnp.abs(z)))) * (1.0 / GLA_TAU)

        _rms_row_groups(h_ref, g_ref, store)

    proj_ref[...] = jnp.dot(xn_ref[...], w_ref[...], preferred_element_type=F32)


def _in_proj(h, g, w_main, w_lr_in, w_lr, b_lr):
    n, d = h.shape
    ncols = w_main.shape[1]
    qk = w_lr.shape[1]
    bm = min(1024, n)
    bn = d // 8
    return pl.pallas_call(
        _in_proj_kernel,
        grid=(n // bm, ncols // bn),
        in_specs=[
            pl.BlockSpec((bm, d), lambda i, j: (i, 0), pipeline_mode=pl.Buffered(1)),
            pl.BlockSpec((1, d), lambda i, j: (0, 0)),
            pl.BlockSpec((d, bn), lambda i, j: (0, j)),
            pl.BlockSpec((d, LANES), lambda i, j: (0, 0)),
            pl.BlockSpec((LANES, qk), lambda i, j: (0, 0)),
            pl.BlockSpec((1, qk), lambda i, j: (0, 0)),
        ],
        out_specs=[
            pl.BlockSpec((bm, bn), lambda i, j: (i, j)),
            pl.BlockSpec((bm, qk), lambda i, j: (i, 0)),
        ],
        out_shape=[jax.ShapeDtypeStruct((n, ncols), F32), jax.ShapeDtypeStruct((n, qk), F32)],
        scratch_shapes=[pltpu.VMEM((bm, d), BF16)],
        compiler_params=_params(("parallel", "arbitrary"), 56),
        name="in_proj",
    )(h, g, w_main, w_lr_in, w_lr, b_lr)


def _gla_kernel(*refs, chunk, n_chunks, has_state, scale):
    if has_state:
        q_ref, k_ref, v_ref, g_ref, la_ref, gg_ref, s0_ref, y_ref, sout_ref, st_ref = refs
    else:
        q_ref, k_ref, v_ref, g_ref, la_ref, gg_ref, y_ref, sout_ref, st_ref = refs
    tt = pl.program_id(2)

    @pl.when(tt == 0)
    def _():
        if has_state:
            st_ref[...] = s0_ref[0, 0].T
        else:
            st_ref[...] = jnp.zeros_like(st_ref)

    row = lax.broadcasted_iota(jnp.int32, (chunk, chunk), 0)
    col = lax.broadcasted_iota(jnp.int32, (chunk, chunk), 1)
    causal = row >= col
    tril = jnp.where(causal, 1.0, 0.0).astype(BF16)
    mid = (chunk - 1) // 2
    nt = (((1,), (1,)), ((), ()))
    tn = (((0,), (0,)), ((), ()))

    for c in range(n_chunks):
        sl = pl.ds(c * chunk, chunk)
        la = la_ref[sl, :]
        la_hi = la.astype(BF16)
        r1 = la - la_hi.astype(F32)
        la_mid = r1.astype(BF16)
        la_lo = (r1 - la_mid.astype(F32)).astype(BF16)
        cb = (jnp.dot(tril, la_hi, preferred_element_type=F32)
              + jnp.dot(tril, la_mid, preferred_element_type=F32)
              + jnp.dot(tril, la_lo, preferred_element_type=F32))
        c_last = cb[chunk - 1:chunk, :]
        c_mid = cb[mid:mid + 1, :]
        q = q_ref[sl, :] * scale
        k = k_ref[sl, :]
        vb = v_ref[sl, :].astype(BF16)
        st = st_ref[...]
        qe = (q * jnp.exp(cb - c_mid)).astype(BF16)
        ke = (k * jnp.exp(c_mid - cb)).astype(BF16)
        att = lax.dot_general(qe, ke, nt, preferred_element_type=F32)
        att = jnp.where(causal, att, 0.0).astype(BF16)
        q2 = (q * jnp.exp(cb)).astype(BF16)
        o = (jnp.dot(att, vb, preferred_element_type=F32)
             + lax.dot_general(q2, st.astype(BF16), nt, preferred_element_type=F32))
        k2 = (k * jnp.exp(c_last - cb)).astype(BF16)
        st_ref[...] = jnp.exp(c_last) * st + lax.dot_general(vb, k2, tn, preferred_element_type=F32)
        o = o * lax.rsqrt(jnp.mean(o * o, axis=-1, keepdims=True) + EPS)
        gate = g_ref[sl, :]
        y_ref[sl, :] = (o * gg_ref[...] * (gate * jax.nn.sigmoid(gate))).astype(y_ref.dtype)

    @pl.when(tt == pl.num_programs(2) - 1)
    def _():
        sout_ref[0, 0] = st_ref[...].T


def _gla(proj, la, gg, s0, batch, seq):
    n = proj.shape[0]
    qk = la.shape[1]
    v_w = gg.shape[1]
    dk = qk // GLA_HEADS
    dv = v_w // GLA_HEADS
    chunk = min(CHUNK, seq)
    tb = min(seq, GLA_ROWS)
    assert seq % tb == 0 and tb % chunk == 0
    nt = seq // tb
    has_state = s0 is not None
    k_blk = qk // dk
    v_blk = 2 * qk // dv
    g_blk = (2 * qk + v_w) // dv
    rows = lambda b, h, t: b * nt + t
    in_specs = [
        pl.BlockSpec((tb, dk), lambda b, h, t: (rows(b, h, t), h)),
        pl.BlockSpec((tb, dk), lambda b, h, t: (rows(b, h, t), k_blk + h)),
        pl.BlockSpec((tb, dv), lambda b, h, t: (rows(b, h, t), v_blk + h)),
        pl.BlockSpec((tb, dv), lambda b, h, t: (rows(b, h, t), g_blk + h)),
        pl.BlockSpec((tb, dk), lambda b, h, t: (rows(b, h, t), h)),
        pl.BlockSpec((1, dv), lambda b, h, t: (0, h)),
    ]
    args = [proj, proj, proj, proj, la, gg]
    if has_state:
        in_specs.append(pl.BlockSpec((1, 1, dk, dv), lambda b, h, t: (b, h, 0, 0)))
        args.append(s0)
    return pl.pallas_call(
        functools.partial(_gla_kernel, chunk=chunk, n_chunks=tb // chunk, has_state=has_state, scale=dk ** -0.5),
        grid=(batch, GLA_HEADS, nt),
        in_specs=in_specs,
        out_specs=[
            pl.BlockSpec((tb, dv), lambda b, h, t: (rows(b, h, t), h)),
            pl.BlockSpec((1, 1, dk, dv), lambda b, h, t: (b, h, 0, 0)),
        ],
        out_shape=[jax.ShapeDtypeStruct((n, v_w), BF16),
                   jax.ShapeDtypeStruct((batch, GLA_HEADS, dk, dv), F32)],
        scratch_shapes=[pltpu.VMEM((dv, dk), F32)],
        compiler_params=_params(("parallel", "parallel", "arbitrary"), 32),
        name="gla",
    )(*args)


def _conv_kernel(*refs, tb, has_state):
    if has_state:
        (ca_ref, cg_ref, sb_ref, scg_ref, sh_ref, wc_ref, bc_ref, lg_ref, lb_ref, ws_ref, hc_ref, hs_ref,
         yc_ref, ys_ref, nc_ref, ns_ref, ubuf, zbuf) = refs
    else:
        (ca_ref, cg_ref, sb_ref, scg_ref, sh_ref, wc_ref, bc_ref, lg_ref, lb_ref, ws_ref,
         yc_ref, ys_ref, nc_ref, ns_ref, ubuf, zbuf) = refs
    tt = pl.program_id(1)
    c_skip = CONF_HIST - (CONF_K - 1)
    s_skip = SC_HIST - (SC_K - 1)

    @pl.when(tt == 0)
    def _():
        ubuf[0:CONF_HIST, :] = jnp.zeros((CONF_HIST, ubuf.shape[1]), F32)
        zbuf[0:SC_HIST, :] = jnp.zeros((SC_HIST, zbuf.shape[1]), F32)
        if has_state:
            ubuf[c_skip:CONF_HIST, :] = hc_ref[0]
            zbuf[s_skip:SC_HIST, :] = hs_ref[0]

    @pl.when(tt > 0)
    def _():
        ubuf[0:CONF_HIST, :] = ubuf[tb:tb + CONF_HIST, :]
        zbuf[0:SC_HIST, :] = zbuf[tb:tb + SC_HIST, :]

    ubuf[CONF_HIST:CONF_HIST + tb, :] = ca_ref[...] * jax.nn.sigmoid(cg_ref[...])
    zbuf[SC_HIST:SC_HIST + tb, :] = scg_ref[...] * sh_ref[...]

    acc = wc_ref[0:1, :] * ubuf[c_skip:c_skip + tb, :]
    for j in range(1, CONF_K):
        acc = acc + wc_ref[j:j + 1, :] * ubuf[c_skip + j:c_skip + j + tb, :]
    c = acc + bc_ref[...]
    mu = jnp.mean(c, axis=-1, keepdims=True)
    cc = c - mu
    var = jnp.mean(cc * cc, axis=-1, keepdims=True)
    y = cc * lax.rsqrt(var + EPS) * lg_ref[...] + lb_ref[...]
    yc_ref[...] = (y * jax.nn.sigmoid(y)).astype(yc_ref.dtype)

    z = ws_ref[0:1, :] * zbuf[s_skip:s_skip + tb, :]
    for j in range(1, SC_K):
        z = z + ws_ref[j:j + 1, :] * zbuf[s_skip + j:s_skip + j + tb, :]
    ys_ref[...] = (sb_ref[...] * z).astype(ys_ref.dtype)

    @pl.when(tt == pl.num_programs(1) - 1)
    def _():
        nc_ref[0] = ubuf[tb + c_skip:tb + CONF_HIST, :]
        ns_ref[0] = zbuf[tb + s_skip:tb + SC_HIST, :]


def _convs(proj, hc, hs, wc, bc, lg, lb, ws, batch, seq):
    n = proj.shape[0]
    w = wc.shape[1]
    tb = min(seq, CONV_ROWS)
    nt = seq // tb
    assert seq % tb == 0 and (nt == 1 or tb >= CONF_HIST)
    has_state = hc is not None
    first = (proj.shape[1] - 5 * w) // w
    rows = lambda b, t: b * nt + t
    col = lambda c: pl.BlockSpec((tb, w), lambda b, t: (rows(b, t), first + c))
    vec = lambda r: pl.BlockSpec((r, w), lambda b, t: (0, 0))
    in_specs = [col(0), col(1), col(2), col(3), col(4), vec(CONF_K), vec(1), vec(1), vec(1), vec(SC_K)]
    args = [proj] * 5 + [wc, bc, lg, lb, ws]
    if has_state:
        in_specs += [pl.BlockSpec((1, CONF_K - 1, w), lambda b, t: (b, 0, 0)),
                     pl.BlockSpec((1, SC_K - 1, w), lambda b, t: (b, 0, 0))]
        args += [hc, hs]
    return pl.pallas_call(
        functools.partial(_conv_kernel, tb=tb, has_state=has_state),
        grid=(batch, nt),
        in_specs=in_specs,
        out_specs=[
            pl.BlockSpec((tb, w), lambda b, t: (rows(b, t), 0)),
            pl.BlockSpec((tb, w), lambda b, t: (rows(b, t), 0)),
            pl.BlockSpec((1, CONF_K - 1, w), lambda b, t: (b, 0, 0)),
            pl.BlockSpec((1, SC_K - 1, w), lambda b, t: (b, 0, 0)),
        ],
        out_shape=[jax.ShapeDtypeStruct((n, w), BF16), jax.ShapeDtypeStruct((n, w), BF16),
                   jax.ShapeDtypeStruct((batch, CONF_K - 1, w), F32),
                   jax.ShapeDtypeStruct((batch, SC_K - 1, w), F32)],
        scratch_shapes=[pltpu.VMEM((CONF_HIST + tb, w), F32), pltpu.VMEM((SC_HIST + tb, w), F32)],
        compiler_params=_params(("parallel", "arbitrary"), 32),
        name="convs",
    )(*args)


def _out_proj_kernel(yg_ref, yc_ref, ys_ref, w_ref, h_ref, o_ref):
    v_w = yg_ref.shape[1]
    c_w = yc_ref.shape[1]
    acc = jnp.dot(yg_ref[...], w_ref[0:v_w, :], preferred_element_type=F32)
    acc = acc + jnp.dot(yc_ref[...], w_ref[v_w:v_w + c_w, :], preferred_element_type=F32)
    acc = acc + jnp.dot(ys_ref[...], w_ref[v_w + c_w:, :], preferred_element_type=F32)
    o_ref[...] = h_ref[...] + acc


def _out_proj(yg, yc, ys, w_out, h):
    n, d = h.shape
    bm = min(1024, n)
    bn = 512
    return pl.pallas_call(
        _out_proj_kernel,
        grid=(n // bm, d // bn),
        in_specs=[
            pl.BlockSpec((bm, yg.shape[1]), lambda i, j: (i, 0)),
            pl.BlockSpec((bm, yc.shape[1]), lambda i, j: (i, 0)),
            pl.BlockSpec((bm, ys.shape[1]), lambda i, j: (i, 0)),
            pl.BlockSpec((w_out.shape[0], bn), lambda i, j: (0, j)),
            pl.BlockSpec((bm, bn), lambda i, j: (i, j)),
        ],
        out_specs=pl.BlockSpec((bm, bn), lambda i, j: (i, j)),
        out_shape=jax.ShapeDtypeStruct((n, d), F32),
        compiler_params=_params(("parallel", "arbitrary"), 48),
        name="out_proj",
    )(yg, yc, ys, w_out, h)


def _scores_kernel(h_ref, g_ref, wq_ref, k1_ref, k2_ref, xnt_ref, s1_ref, s2_ref):
    @pl.when(pl.program_id(1) == 0)
    def _():
        def store(rows, xn):
            xnt_ref[:, rows] = xn.T.astype(BF16)

        _rms_row_groups(h_ref, g_ref, store)

    q = jnp.dot(wq_ref[...], xnt_ref[...], preferred_element_type=F32)
    half = q.shape[0] // 2
    s1_ref[0] = jnp.dot(k1_ref[0], q[:half].astype(BF16), preferred_element_type=F32)
    s2_ref[0] = jnp.dot(k2_ref[0], q[half:].astype(BF16), preferred_element_type=F32)


def _scores(h, g, wq_t, k1, k2):
    n, d = h.shape
    bt = min(512, n)
    dq = wq_t.shape[0] // PEER_HEADS
    nk = k1.shape[1]
    return pl.pallas_call(
        _scores_kernel,
        grid=(n // bt, PEER_HEADS),
        in_specs=[
            pl.BlockSpec((bt, d), lambda i, hd: (i, 0)),
            pl.BlockSpec((1, d), lambda i, hd: (0, 0)),
            pl.BlockSpec((dq, d), lambda i, hd: (hd, 0)),
            pl.BlockSpec((1, nk, dq // 2), lambda i, hd: (hd, 0, 0)),
            pl.BlockSpec((1, nk, dq // 2), lambda i, hd: (hd, 0, 0)),
        ],
        out_specs=[
            pl.BlockSpec((d, bt), lambda i, hd: (0, i)),
            pl.BlockSpec((1, nk, bt), lambda i, hd: (hd, 0, i)),
            pl.BlockSpec((1, nk, bt), lambda i, hd: (hd, 0, i)),
        ],
        out_shape=[jax.ShapeDtypeStruct((d, n), BF16),
                   jax.ShapeDtypeStruct((PEER_HEADS, nk, n), F32),
                   jax.ShapeDtypeStruct((PEER_HEADS, nk, n), F32)],
        compiler_params=_params(("parallel", "arbitrary"), 48),
        name="peer_scores",
    )(h, g, wq_t, k1, k2)


def _oddeven_merge_sort_pairs(n):
    pairs = []
    p = 1
    while p < n:
        k = p
        while k >= 1:
            for j in range(k % p, n - k, 2 * k):
                for i in range(min(k, n - j - k)):
                    if (i + j) // (2 * p) == (i + j + k) // (2 * p):
                        pairs.append((i + j, i + j + k))
            k //= 2
        p *= 2
    return pairs


def _bitonic_merge_pairs(n):
    pairs = []
    k = n // 2
    while k >= 1:
        pairs += [(i, i + k) for i in range(n) if (i // k) % 2 == 0]
        k //= 2
    return pairs


_SORT_PAIRS = _oddeven_merge_sort_pairs(PEER_TOPK)
_MERGE_PAIRS = _bitonic_merge_pairs(PEER_TOPK)


def _exchange(v, pairs):
    v = list(v)
    for i, j in pairs:
        v[i], v[j] = jnp.maximum(v[i], v[j]), jnp.minimum(v[i], v[j])
    return v


def _key_tile(a):
    start = a * PEER_HEADS
    return pl.ds(start if isinstance(start, int) else pl.multiple_of(start, PEER_HEADS), PEER_HEADS)


def _top_values(s_ref, s_scr, g_scr, v_scr):
    nh, nk, k = PEER_HEADS, PEER_NKEYS, PEER_TOPK
    n_groups = nk // k
    for grp in range(n_groups):
        vals = []
        for i in range(k):
            a = grp * k + i
            t = s_ref[pl.ds(a, nh, stride=nk), :]
            s_scr[a * nh:(a + 1) * nh, :] = t
            vals.append(t)
        for i, t in enumerate(_exchange(vals, _SORT_PAIRS)):
            g_scr[(grp * k + i) * nh:(grp * k + i + 1) * nh, :] = t
    span = 1
    while span < n_groups:
        for grp in range(0, n_groups, 2 * span):
            lo = [g_scr[(grp * k + i) * nh:(grp * k + i + 1) * nh, :] for i in range(k)]
            hi = [g_scr[((grp + span) * k + i) * nh:((grp + span) * k + i + 1) * nh, :] for i in range(k)]
            top = _exchange([jnp.maximum(lo[i], hi[k - 1 - i]) for i in range(k)], _MERGE_PAIRS)
            for i, t in enumerate(top):
                g_scr[(grp * k + i) * nh:(grp * k + i + 1) * nh, :] = t
        span *= 2
    for i in range(k):
        v_scr[i] = g_scr[i * nh:(i + 1) * nh, :]


def _ties(s_scr, v_scr):
    k = PEER_TOPK
    v = [v_scr[i] for i in range(k)]
    bad = jnp.zeros_like(v[0])
    for i in range(k - 1):
        bad = bad + jnp.where(v[i] == v[i + 1], 1.0, 0.0)

    def count(a, c):
        return c + jnp.where(s_scr[_key_tile(a), :] >= v[k - 1], 1.0, 0.0)

    n_ge = lax.fori_loop(0, PEER_NKEYS, count, jnp.zeros_like(v[0]), unroll=8)
    return bad + jnp.where(n_ge == float(k), 0.0, 1.0)


def _rank_keys(s_scr, r_scr):
    nh, nk = PEER_HEADS, PEER_NKEYS
    bt = s_scr.shape[1]

    def init(a, m):
        r_scr[_key_tile(a), :] = jnp.full((nh, bt), NOT_SELECTED, F32)
        return jnp.maximum(m, s_scr[_key_tile(a), :])

    m0 = lax.fori_loop(0, nk, init, jnp.full((nh, bt), -jnp.inf, F32), unroll=8)

    def round_body(r, m):
        def find(a, idx):
            return jnp.minimum(idx, jnp.where(s_scr[_key_tile(a), :] == m, lax.convert_element_type(a, F32),
                                              float(nk)))

        idx = lax.fori_loop(0, nk, find, jnp.full((nh, bt), float(nk), F32), unroll=8)
        rank = lax.convert_element_type(r, F32)

        def knock(a, m_next):
            hit = idx == lax.convert_element_type(a, F32)
            s_new = jnp.where(hit, -jnp.inf, s_scr[_key_tile(a), :])
            s_scr[_key_tile(a), :] = s_new
            r_scr[_key_tile(a), :] = jnp.where(hit, rank, r_scr[_key_tile(a), :])
            return jnp.maximum(m_next, s_new)

        return lax.fori_loop(0, nk, knock, jnp.full((nh, bt), -jnp.inf, F32), unroll=8)

    lax.fori_loop(0, PEER_TOPK, round_body, m0)


def _topk_kernel(s1_ref, s2_ref, c1_ref, e1_ref, r2_ref, e2_ref,
                 s1_scr, s2_scr, g_scr, r2_scr, e2k_scr, v1_scr, v2_scr, c_scr, e1z_scr, e2_scr, t_scr):
    nh, nk, k = PEER_HEADS, PEER_NKEYS, PEER_TOPK
    _top_values(s1_ref, s1_scr, g_scr, v1_scr)
    _top_values(s2_ref, s2_scr, g_scr, v2_scr)

    v1 = [v1_scr[i] for i in range(PEER_TOPK)]
    v2 = [v2_scr[j] for j in range(PEER_TOPK)]
    cand = {(i, j): v1[i] + v2[j] for (i, j) in _CELLS}
    before = {p: 0.0 for p in _CELLS}
    for x, p in enumerate(_CELLS):
        for q in _CELLS[x + 1:]:
            if p[0] <= q[0] and p[1] <= q[1]:
                before[q] = before[q] + 1.0
            elif not (q[0] <= p[0] and q[1] <= p[1]):
                p_first = jnp.where(cand[p] >= cand[q], 1.0, 0.0)
                before[q] = before[q] + p_first
                before[p] = before[p] + (1.0 - p_first)
    e1 = [jnp.exp(v1[i] - v1[0]) for i in range(PEER_TOPK)]
    e2 = [jnp.exp(v2[j] - v2[0]) for j in range(PEER_TOPK)]
    height = [jnp.zeros_like(v1[0]) for _ in range(PEER_TOPK)]
    z = jnp.zeros_like(v1[0])
    for (i, j) in _CELLS:
        sel = jnp.where(before[(i, j)] < float(PEER_TOPK), 1.0, 0.0)
        height[i] = height[i] + sel
        z = z + sel * (e1[i] * e2[j])
    inv_z = 1.0 / z
    for i in range(PEER_TOPK):
        c_scr[i] = height[i]
        e1z_scr[i] = e1[i] * inv_z
        e2_scr[i] = e2[i]
    for c in range(1, k + 1):
        t = jnp.full_like(v1[0], jnp.inf)
        for i in range(k):
            t = jnp.minimum(t, jnp.where(height[i] >= float(c), v1[i], jnp.inf))
        t_scr[c - 1] = t

    n_ties = jnp.sum(_ties(s1_scr, v1_scr) + _ties(s2_scr, v2_scr))

    @pl.when(n_ties == 0.0)
    def _():
        v1_max = v1_scr[0]
        v2_max = v2_scr[0]
        inv = e1z_scr[0]

        def emit1(a, carry):
            s = s1_scr[_key_tile(a), :]
            c = jnp.zeros_like(s)
            for lvl in range(k):
                c = c + jnp.where(s >= t_scr[lvl], 1.0, 0.0)
            c1_ref[a] = c
            e1_ref[a] = jnp.exp(s - v1_max) * inv
            return carry

        lax.fori_loop(0, nk, emit1, 0, unroll=4)

        def emit2(b, carry):
            s = s2_scr[_key_tile(b), :]
            r = jnp.zeros_like(s)
            for i in range(k):
                r = r + jnp.where(v2_scr[i] > s, 1.0, 0.0)
            r2_scr[_key_tile(b), :] = r
            e2k_scr[_key_tile(b), :] = jnp.exp(s - v2_max)
            return carry

        lax.fori_loop(0, nk, emit2, 0, unroll=4)

    @pl.when(n_ties != 0.0)
    def _():
        _rank_keys(s1_scr, g_scr)

        def emit1(a, carry):
            rk = g_scr[_key_tile(a), :]
            c = jnp.zeros_like(rk)
            e = jnp.zeros_like(rk)
            for r in range(k):
                eq = rk == float(r)
                c = jnp.where(eq, c_scr[r], c)
                e = jnp.where(eq, e1z_scr[r], e)
            c1_ref[a] = c
            e1_ref[a] = e
            return carry

        lax.fori_loop(0, nk, emit1, 0, unroll=4)
        _rank_keys(s2_scr, r2_scr)

        def emit2(b, carry):
            rk = r2_scr[_key_tile(b), :]
            e = jnp.zeros_like(rk)
            for r in range(k):
                e = jnp.where(rk == float(r), e2_scr[r], e)
            e2k_scr[_key_tile(b), :] = e
            return carry

        lax.fori_loop(0, nk, emit2, 0, unroll=4)

    for hd in range(nh):
        r2_ref[hd] = r2_scr[pl.ds(hd, nk, stride=nh), :]
        e2_ref[hd] = e2k_scr[pl.ds(hd, nk, stride=nh), :]


def _topk(s1, s2):
    nh, nk, n = s1.shape
    bt = LANES
    rows = nh * nk
    big = pltpu.VMEM((rows, bt), F32)
    small = pltpu.VMEM((PEER_TOPK, nh, bt), F32)
    return pl.pallas_call(
        _topk_kernel,
        grid=(n // bt,),
        in_specs=[pl.BlockSpec((rows, bt), lambda i: (0, i)), pl.BlockSpec((rows, bt), lambda i: (0, i))],
        out_specs=[
            pl.BlockSpec((nk, nh, bt), lambda i: (0, 0, i)),
            pl.BlockSpec((nk, nh, bt), lambda i: (0, 0, i)),
            pl.BlockSpec((nh, nk, bt), lambda i: (0, 0, i)),
            pl.BlockSpec((nh, nk, bt), lambda i: (0, 0, i)),
        ],
        out_shape=[jax.ShapeDtypeStruct((nk, nh, n), F32), jax.ShapeDtypeStruct((nk, nh, n), F32),
                   jax.ShapeDtypeStruct((nh, nk, n), F32), jax.ShapeDtypeStruct((nh, nk, n), F32)],
        scratch_shapes=[big] * 5 + [small] * 6,
        compiler_params=_params(("parallel",), 40),
        name="peer_topk",
    )(s1.reshape(rows, n), s2.reshape(rows, n))


def _peer_ffn_kernel(xnt_ref, u_ref, v_ref, c1_ref, e1_ref, r2_ref, e2_ref, out_ref, w_scr):
    nk = PEER_NKEYS
    j = pl.program_id(1)
    last = pl.num_programs(1) - 1
    bt = out_ref.shape[1]

    def accumulate():
        out_ref[...] += lax.dot_general(v_ref[...], w_scr[(j + 1) % 2], (((0,), (0,)), ((), ())),
                                        preferred_element_type=F32)

    def weigh():
        at = jnp.dot(u_ref[...], xnt_ref[...], preferred_element_type=F32)
        cur = j % 2
        for al in range(A_PER_BLOCK):
            rows = slice(al * nk, (al + 1) * nk)
            for l0 in range(0, bt, LANES):
                lanes = slice(l0, l0 + LANES)
                g = jnp.zeros((nk, LANES), F32)
                for hd in range(PEER_HEADS):
                    sel = r2_ref[hd, :, lanes] < c1_ref[al, hd:hd + 1, lanes]
                    g = g + jnp.where(sel, e1_ref[al, hd:hd + 1, lanes] * e2_ref[hd, :, lanes], 0.0)
                w_scr[cur, rows, lanes] = (g * jax.nn.gelu(at[rows, lanes])).astype(BF16)

    @pl.when(j == 0)
    def _():
        out_ref[...] = jnp.zeros_like(out_ref)
        weigh()

    @pl.when(jnp.logical_and(j > 0, j < last))
    def _():
        accumulate()
        weigh()

    @pl.when(j == last)
    def _():
        accumulate()


def _peer_ffn(xnt, u, v, c1, e1, r2, e2):
    d, n = xnt.shape
    n_exp = u.shape[0]
    nk, nh, _ = c1.shape
    bt = min(512, n)
    be = A_PER_BLOCK * nk
    nj = n_exp // be
    once = pl.Buffered(1)
    return pl.pallas_call(
        _peer_ffn_kernel,
        grid=(n // bt, nj + 1),
        in_specs=[
            pl.BlockSpec((d, bt), lambda i, j: (0, i), pipeline_mode=once),
            pl.BlockSpec((be, d), lambda i, j: (jnp.minimum(j, nj - 1), 0)),
            pl.BlockSpec((be, d), lambda i, j: (jnp.maximum(j - 1, 0), 0)),
            pl.BlockSpec((A_PER_BLOCK, nh, bt), lambda i, j: (jnp.minimum(j, nj - 1), 0, i)),
            pl.BlockSpec((A_PER_BLOCK, nh, bt), lambda i, j: (jnp.minimum(j, nj - 1), 0, i)),
            pl.BlockSpec((nh, nk, bt), lambda i, j: (0, 0, i), pipeline_mode=once),
            pl.BlockSpec((nh, nk, bt), lambda i, j: (0, 0, i), pipeline_mode=once),
        ],
        out_specs=pl.BlockSpec((d, bt), lambda i, j: (0, i)),
        out_shape=jax.ShapeDtypeStruct((d, n), F32),
        scratch_shapes=[pltpu.VMEM((2, be, bt), BF16)],
        compiler_params=_params(("parallel", "arbitrary"), 56),
        name="peer_ffn",
    )(xnt, u, v, c1, e1, r2, e2)


def _add_t_kernel(h_ref, ot_ref, o_ref):
    o_ref[...] = h_ref[...] + ot_ref[...].T


def _add_transposed(h, out_t):
    n, d = h.shape
    bm = min(256, n)
    return pl.pallas_call(
        _add_t_kernel,
        grid=(n // bm,),
        in_specs=[pl.BlockSpec((bm, d), lambda i: (i, 0)), pl.BlockSpec((d, bm), lambda i: (0, i))],
        out_specs=pl.BlockSpec((bm, d), lambda i: (i, 0)),
        out_shape=jax.ShapeDtypeStruct((n, d), F32),
        compiler_params=_params(("parallel",), 56),
        name="add_transposed",
    )(h, out_t)


def _ple_kernel(h_ref, g_ref, wg_ref, p_ref, wp_ref, hcol_ref, o_ref, xn_ref):
    @pl.when(pl.program_id(1) == 0)
    def _():
        def store(rows, xn):
            xn_ref[rows, :] = xn.astype(BF16)

        _rms_row_groups(h_ref, g_ref, store)

    gate = jax.nn.sigmoid(jnp.dot(xn_ref[...], wg_ref[...], preferred_element_type=F32))
    pp = jnp.dot(p_ref[...], wp_ref[...], preferred_element_type=F32)
    o_ref[...] = hcol_ref[...] + pp * gate


def _ple(h, g, w_gate, p, w_proj):
    n, d = h.shape
    bm = min(1024, n)
    bn = 512
    pd = p.shape[1]
    return pl.pallas_call(
        _ple_kernel,
        grid=(n // bm, d // bn),
        in_specs=[
            pl.BlockSpec((bm, d), lambda i, j: (i, 0), pipeline_mode=pl.Buffered(1)),
            pl.BlockSpec((1, d), lambda i, j: (0, 0)),
            pl.BlockSpec((d, bn), lambda i, j: (0, j)),
            pl.BlockSpec((bm, pd), lambda i, j: (i, 0)),
            pl.BlockSpec((pd, bn), lambda i, j: (0, j)),
            pl.BlockSpec((bm, bn), lambda i, j: (i, j)),
        ],
        out_specs=pl.BlockSpec((bm, bn), lambda i, j: (i, j)),
        out_shape=jax.ShapeDtypeStruct((n, d), F32),
        scratch_shapes=[pltpu.VMEM((bm, d), BF16)],
        compiler_params=_params(("parallel", "arbitrary"), 56),
        name="ple",
    )(h, g, w_gate, p, w_proj, h)


def _final_norm_kernel(h_ref, g_ref, o_ref):
    def store(rows, xn):
        o_ref[rows, :] = xn

    _rms_row_groups(h_ref, g_ref, store)


def _final_norm(h, g):
    n, d = h.shape
    bm = min(512, n)
    return pl.pallas_call(
        _final_norm_kernel,
        grid=(n // bm,),
        in_specs=[pl.BlockSpec((bm, d), lambda i: (i, 0)), pl.BlockSpec((1, d), lambda i: (0, 0))],
        out_specs=pl.BlockSpec((bm, d), lambda i: (i, 0)),
        out_shape=jax.ShapeDtypeStruct((n, d), F32),
        compiler_params=_params(("parallel",), 48),
        name="final_norm",
    )(h, g)


def _cast_kernel(x_ref, o_ref):
    o_ref[...] = x_ref[...].astype(o_ref.dtype)


def _cast_layer(x, layer):
    _, r, c = x.shape
    bm = min(CAST_ROWS, r)
    return pl.pallas_call(
        _cast_kernel,
        grid=(r // bm,),
        in_specs=[pl.BlockSpec((None, bm, c), lambda i: (layer, i, 0))],
        out_specs=pl.BlockSpec((bm, c), lambda i: (i, 0)),
        out_shape=jax.ShapeDtypeStruct((r, c), BF16),
        compiler_params=_params(("parallel",), 40),
        name="cast_weight",
    )(x)


def _w_in_kernel(w_ref, main_ref, lr_ref, *, lr0):
    x = w_ref[...]
    main_ref[:, :lr0] = x[:, :lr0].astype(BF16)
    main_ref[:, lr0:] = x[:, lr0 + GLA_LR:].astype(BF16)
    lr_ref[...] = jnp.zeros_like(lr_ref)
    lr_ref[:, :GLA_LR] = x[:, lr0:lr0 + GLA_LR].astype(BF16)


def _split_w_in(w_in, layer):
    _, d, cols = w_in.shape
    lr0 = 2 * (d // 4) + 2 * (d // 2)
    bm = min(CAST_ROWS // 4, d)
    return pl.pallas_call(
        functools.partial(_w_in_kernel, lr0=lr0),
        grid=(d // bm,),
        in_specs=[pl.BlockSpec((None, bm, cols), lambda i: (layer, i, 0))],
        out_specs=[pl.BlockSpec((bm, cols - GLA_LR), lambda i: (i, 0)), pl.BlockSpec((bm, LANES), lambda i: (i, 0))],
        out_shape=[jax.ShapeDtypeStruct((d, cols - GLA_LR), BF16), jax.ShapeDtypeStruct((d, LANES), BF16)],
        compiler_params=_params(("parallel",), 48),
        name="split_w_in",
    )(w_in)


def _prep_layer(i, g_mix, w_in, gla_w_lr, gla_b_lr, gla_g_out, conf_w_dw, conf_b_dw, conf_ln_g, conf_ln_b,
                sc_w_dw, w_out, g_ffn, peer_w_q, peer_k1, peer_k2, peer_u, peer_v, g_ple, ple_w_gate, ple_w_proj):
    row = lambda x: x[i][None, :]
    w_main, w_lr_in = _split_w_in(w_in, i)
    return dict(
        g_mix=row(g_mix), w_main=w_main, w_lr_in=w_lr_in,
        w_lr=jnp.pad(gla_w_lr[i], ((0, LANES - GLA_LR), (0, 0))).astype(BF16),
        b_lr=row(gla_b_lr), gg=row(gla_g_out),
        wc=conf_w_dw[i], bc=row(conf_b_dw), lg=row(conf_ln_g), lb=row(conf_ln_b), ws=sc_w_dw[i],
        w_out=_cast_layer(w_out, i),
        g_ffn=row(g_ffn), wq_t=peer_w_q[i].T.astype(BF16),
        k1=peer_k1[i].astype(BF16), k2=peer_k2[i].astype(BF16),
        u=_cast_layer(peer_u, i), v=_cast_layer(peer_v, i),
        g_ple=row(g_ple), w_gate=_cast_layer(ple_w_gate, i), w_proj=ple_w_proj[i].astype(BF16),
    )


def _layer(h, p, s_gla, s_conf, s_sc, lw, batch, seq):
    proj, la = _in_proj(h, lw["g_mix"], lw["w_main"], lw["w_lr_in"], lw["w_lr"], lw["b_lr"])
    y_gla, new_gla = _gla(proj, la, lw["gg"], s_gla, batch, seq)
    y_conf, y_sc, new_conf, new_sc = _convs(proj, s_conf, s_sc, lw["wc"], lw["bc"], lw["lg"], lw["lb"], lw["ws"],
                                            batch, seq)
    h = _out_proj(y_gla, y_conf, y_sc, lw["w_out"], h)
    xnt, s1, s2 = _scores(h, lw["g_ffn"], lw["wq_t"], lw["k1"], lw["k2"])
    c1, e1, r2, e2 = _topk(s1, s2)
    h = _add_transposed(h, _peer_ffn(xnt, lw["u"], lw["v"], c1, e1, r2, e2))
    h = _ple(h, lw["g_ple"], lw["w_gate"], p.astype(BF16), lw["w_proj"])
    return h, new_gla, new_conf, new_sc


def _trunk(x, p, s_gla, s_conf, s_sc, layers, g_final):
    batch, seq, d = x.shape
    h = x.reshape(batch * seq, d)
    new_gla, new_conf, new_sc = [], [], []
    for i, lw in enumerate(layers):
        st = (None, None, None) if s_gla is None else (s_gla[i], s_conf[i], s_sc[i])
        h, sg, scf, ssc = _layer(h, p[i].reshape(batch * seq, -1), *st, lw, batch, seq)
        new_gla.append(sg)
        new_conf.append(scf)
        new_sc.append(ssc)
    y = _final_norm(h, g_final[None, :]).reshape(batch, seq, d)
    return y, jnp.stack(new_gla), jnp.stack(new_conf), jnp.stack(new_sc)


def kernel(x_prompt, x_sample, state_gla, state_conf, state_sconv, p_prompt, p_sample, g_mix, w_in, gla_w_lr, gla_b_lr, gla_g_out, conf_w_dw, conf_b_dw, conf_ln_g, conf_ln_b, sc_w_dw, w_out, g_ffn, peer_w_q, peer_k1, peer_k2, peer_u, peer_v, g_ple, ple_w_gate, ple_w_proj, g_final):
    layers = [_prep_layer(i, g_mix, w_in, gla_w_lr, gla_b_lr, gla_g_out, conf_w_dw, conf_b_dw, conf_ln_g,
                          conf_ln_b, sc_w_dw, w_out, g_ffn, peer_w_q, peer_k1, peer_k2, peer_u, peer_v, g_ple,
                          ple_w_gate, ple_w_proj) for i in range(w_in.shape[0])]
    y_p, gla_p, conf_p, sc_p = _trunk(x_prompt, p_prompt, None, None, None, layers, g_final)
    y_s, gla_s, conf_s, sc_s = _trunk(x_sample, p_sample, state_gla, state_conf, state_sconv, layers, g_final)
    return (y_p, y_s, gla_p, conf_p, sc_p, gla_s, conf_s, sc_s)
```

```python
import functools

import jax
import jax.numpy as jnp
from jax import lax
from jax.experimental import pallas as pl
from jax.experimental.pallas import tpu as pltpu

F32 = jnp.float32
BF16 = jnp.bfloat16

EPS = 1e-6
CHUNK = 64
GLA_HEADS = 4
GLA_LR = 16
GLA_TAU = 16.0
CONF_K = 31
SC_K = 3
PEER_HEADS = 8
PEER_NKEYS = 128
PEER_TOPK = 16

LANES = 128
SUBLANES = 8
CONF_HIST = 32
SC_HIST = 8
NORM_ROWS = 128
CAST_ROWS = 512
GLA_ROWS = 256
CONV_ROWS = 256
A_PER_BLOCK = 4
NOT_SELECTED = 64.0
MIB = 1024 * 1024

_CELLS = [(i, j) for i in range(PEER_TOPK) for j in range(PEER_TOPK) if (i + 1) * (j + 1) <= PEER_TOPK]


def _params(semantics, vmem_mib, flags=None):
    return pltpu.CompilerParams(dimension_semantics=semantics, vmem_limit_bytes=vmem_mib * MIB, flags=flags)


def _rms(x, g):
    ms = jnp.mean(x * x, axis=-1, keepdims=True)
    return x * lax.rsqrt(ms + EPS) * g


def _rms_row_groups(h_ref, g_ref, store):
    g = g_ref[...]
    for r0 in range(0, h_ref.shape[0], NORM_ROWS):
        rows = slice(r0, min(r0 + NORM_ROWS, h_ref.shape[0]))
        store(rows, _rms(h_ref[rows, :], g))


def _in_proj_kernel(h_ref, g_ref, w_ref, wlr_in_ref, wlr_ref, blr_ref, proj_ref, la_ref, xn_ref):
    nt = (((1,), (1,)), ((), ()))

    @pl.when(pl.program_id(1) == 0)
    def _():
        def store(rows, xn):
            xn = xn.astype(BF16)
            xn_ref[rows, :] = xn
            lr = lax.dot_general(xn, wlr_in_ref[...], nt, preferred_element_type=F32)
            z = jnp.dot(lr.astype(BF16), wlr_ref[...], preferred_element_type=F32) + blr_ref[...]
            la_ref[rows, :] = (jnp.minimum(z, 0.0) - jnp.log1p(jnp.exp(-jnp.abs(z)))) * (1.0 / GLA_TAU)

        _rms_row_groups(h_ref, g_ref, store)

    proj_ref[...] = lax.dot_general(xn_ref[...], w_ref[...], nt, preferred_element_type=F32)


def _in_proj(h, g, w_main, w_lr_in, w_lr, b_lr):
    n, d = h.shape
    ncols = w_main.shape[0]
    qk = w_lr.shape[1]
    bm = min(1024, n)
    bn = d // 8
    return pl.pallas_call(
        _in_proj_kernel,
        grid=(n // bm, ncols // bn),
        in_specs=[
            pl.BlockSpec((bm, d), lambda i, j: (i, 0), pipeline_mode=pl.Buffered(1)),
            pl.BlockSpec((1, d), lambda i, j: (0, 0)),
            pl.BlockSpec((bn, d), lambda i, j: (j, 0)),
            pl.BlockSpec((LANES, d), lambda i, j: (0, 0)),
            pl.BlockSpec((LANES, qk), lambda i, j: (0, 0)),
            pl.BlockSpec((1, qk), lambda i, j: (0, 0)),
        ],
        out_specs=[
            pl.BlockSpec((bm, bn), lambda i, j: (i, j)),
            pl.BlockSpec((bm, qk), lambda i, j: (i, 0)),
        ],
        out_shape=[jax.ShapeDtypeStruct((n, ncols), F32), jax.ShapeDtypeStruct((n, qk), F32)],
        scratch_shapes=[pltpu.VMEM((bm, d), BF16)],
        compiler_params=_params(("parallel", "arbitrary"), 56),
        name="in_proj",
    )(h, g, w_main, w_lr_in, w_lr, b_lr)


def _gla_kernel(*refs, chunk, n_chunks, has_state, scale):
    if has_state:
        q_ref, k_ref, v_ref, g_ref, la_ref, gg_ref, s0_ref, y_ref, sout_ref, st_ref = refs
    else:
        q_ref, k_ref, v_ref, g_ref, la_ref, gg_ref, y_ref, sout_ref, st_ref = refs
    tt = pl.program_id(1)
    n_heads, dv, dk = st_ref.shape

    @pl.when(tt == 0)
    def _():
        for hd in range(n_heads):
            st_ref[hd] = s0_ref[0, hd].T if has_state else jnp.zeros((dv, dk), F32)

    row = lax.broadcasted_iota(jnp.int32, (chunk, chunk), 0)
    col = lax.broadcasted_iota(jnp.int32, (chunk, chunk), 1)
    causal = row >= col
    tril = jnp.where(causal, 1.0, 0.0).astype(BF16)
    mid = (chunk - 1) // 2
    nt = (((1,), (1,)), ((), ()))
    tn = (((0,), (0,)), ((), ()))

    for c in range(n_chunks):
        sl = pl.ds(c * chunk, chunk)
        la = la_ref[sl, :]
        la_hi = la.astype(BF16)
        r1 = la - la_hi.astype(F32)
        la_mid = r1.astype(BF16)
        la_lo = (r1 - la_mid.astype(F32)).astype(BF16)
        cb_all = (jnp.dot(tril, la_hi, preferred_element_type=F32)
                  + jnp.dot(tril, la_mid, preferred_element_type=F32)
                  + jnp.dot(tril, la_lo, preferred_element_type=F32))
        for hd in range(n_heads):
            kc = slice(hd * dk, (hd + 1) * dk)
            vc = slice(hd * dv, (hd + 1) * dv)
            cb = cb_all[:, kc]
            c_last = cb[chunk - 1:chunk, :]
            c_mid = cb[mid:mid + 1, :]
            q = q_ref[sl, kc] * scale
            k = k_ref[sl, kc]
            vb = v_ref[sl, vc].astype(BF16)
            st = st_ref[hd]
            qe = (q * jnp.exp(cb - c_mid)).astype(BF16)
            ke = (k * jnp.exp(c_mid - cb)).astype(BF16)
            att = lax.dot_general(qe, ke, nt, preferred_element_type=F32)
            att = jnp.where(causal, att, 0.0).astype(BF16)
            q2 = (q * jnp.exp(cb)).astype(BF16)
            o = (jnp.dot(att, vb, preferred_element_type=F32)
                 + lax.dot_general(q2, st.astype(BF16), nt, preferred_element_type=F32))
            k2 = (k * jnp.exp(c_last - cb)).astype(BF16)
            st_ref[hd] = jnp.exp(c_last) * st + lax.dot_general(vb, k2, tn, preferred_element_type=F32)
            o = o * lax.rsqrt(jnp.mean(o * o, axis=-1, keepdims=True) + EPS)
            gate = g_ref[sl, vc]
            y_ref[sl, vc] = (o * gg_ref[:, vc] * (gate * jax.nn.sigmoid(gate))).astype(y_ref.dtype)

    @pl.when(tt == pl.num_programs(1) - 1)
    def _():
        for hd in range(n_heads):
            sout_ref[0, hd] = st_ref[hd].T


def _gla(proj, la, gg, s0, batch, seq):
    n = proj.shape[0]
    qk = la.shape[1]
    v_w = gg.shape[1]
    dk = qk // GLA_HEADS
    dv = v_w // GLA_HEADS
    chunk = min(CHUNK, seq)
    tb = min(seq, GLA_ROWS)
    assert seq % tb == 0 and tb % chunk == 0 and 2 * qk % v_w == 0
    nt = seq // tb
    has_state = s0 is not None
    rows = lambda b, t: b * nt + t
    v_blk = 2 * qk // v_w
    in_specs = [
        pl.BlockSpec((tb, qk), lambda b, t: (rows(b, t), 0)),
        pl.BlockSpec((tb, qk), lambda b, t: (rows(b, t), 1)),
        pl.BlockSpec((tb, v_w), lambda b, t: (rows(b, t), v_blk)),
        pl.BlockSpec((tb, v_w), lambda b, t: (rows(b, t), v_blk + 1)),
        pl.BlockSpec((tb, qk), lambda b, t: (rows(b, t), 0)),
        pl.BlockSpec((1, v_w), lambda b, t: (0, 0)),
    ]
    args = [proj, proj, proj, proj, la, gg]
    if has_state:
        in_specs.append(pl.BlockSpec((1, GLA_HEADS, dk, dv), lambda b, t: (b, 0, 0, 0)))
        args.append(s0)
    return pl.pallas_call(
        functools.partial(_gla_kernel, chunk=chunk, n_chunks=tb // chunk, has_state=has_state, scale=dk ** -0.5),
        grid=(batch, nt),
        in_specs=in_specs,
        out_specs=[
            pl.BlockSpec((tb, v_w), lambda b, t: (rows(b, t), 0)),
            pl.BlockSpec((1, GLA_HEADS, dk, dv), lambda b, t: (b, 0, 0, 0)),
        ],
        out_shape=[jax.ShapeDtypeStruct((n, v_w), BF16),
                   jax.ShapeDtypeStruct((batch, GLA_HEADS, dk, dv), F32)],
        scratch_shapes=[pltpu.VMEM((GLA_HEADS, dv, dk), F32)],
        compiler_params=_params(("parallel", "arbitrary"), 48),
        name="gla",
    )(*args)


def _conv_kernel(*refs, tb, has_state):
    if has_state:
        (ca_ref, cg_ref, sb_ref, scg_ref, sh_ref, wc_ref, bc_ref, lg_ref, lb_ref, ws_ref, hc_ref, hs_ref,
         yc_ref, ys_ref, nc_ref, ns_ref, ubuf, zbuf) = refs
    else:
        (ca_ref, cg_ref, sb_ref, scg_ref, sh_ref, wc_ref, bc_ref, lg_ref, lb_ref, ws_ref,
         yc_ref, ys_ref, nc_ref, ns_ref, ubuf, zbuf) = refs
    tt = pl.program_id(1)
    c_skip = CONF_HIST - (CONF_K - 1)
    s_skip = SC_HIST - (SC_K - 1)

    @pl.when(tt == 0)
    def _():
        ubuf[0:CONF_HIST, :] = jnp.zeros((CONF_HIST, ubuf.shape[1]), F32)
        zbuf[0:SC_HIST, :] = jnp.zeros((SC_HIST, zbuf.shape[1]), F32)
        if has_state:
            ubuf[c_skip:CONF_HIST, :] = hc_ref[0]
            zbuf[s_skip:SC_HIST, :] = hs_ref[0]

    @pl.when(tt > 0)
    def _():
        ubuf[0:CONF_HIST, :] = ubuf[tb:tb + CONF_HIST, :]
        zbuf[0:SC_HIST, :] = zbuf[tb:tb + SC_HIST, :]

    ubuf[CONF_HIST:CONF_HIST + tb, :] = ca_ref[...] * jax.nn.sigmoid(cg_ref[...])
    zbuf[SC_HIST:SC_HIST + tb, :] = scg_ref[...] * sh_ref[...]

    acc = wc_ref[0:1, :] * ubuf[c_skip:c_skip + tb, :]
    for j in range(1, CONF_K):
        acc = acc + wc_ref[j:j + 1, :] * ubuf[c_skip + j:c_skip + j + tb, :]
    c = acc + bc_ref[...]
    mu = jnp.mean(c, axis=-1, keepdims=True)
    cc = c - mu
    var = jnp.mean(cc * cc, axis=-1, keepdims=True)
    y = cc * lax.rsqrt(var + EPS) * lg_ref[...] + lb_ref[...]
    yc_ref[...] = (y * jax.nn.sigmoid(y)).astype(yc_ref.dtype)

    z = ws_ref[0:1, :] * zbuf[s_skip:s_skip + tb, :]
    for j in range(1, SC_K):
        z = z + ws_ref[j:j + 1, :] * zbuf[s_skip + j:s_skip + j + tb, :]
    ys_ref[...] = (sb_ref[...] * z).astype(ys_ref.dtype)

    @pl.when(tt == pl.num_programs(1) - 1)
    def _():
        nc_ref[0] = ubuf[tb + c_skip:tb + CONF_HIST, :]
        ns_ref[0] = zbuf[tb + s_skip:tb + SC_HIST, :]


def _convs(proj, hc, hs, wc, bc, lg, lb, ws, batch, seq):
    n = proj.shape[0]
    w = wc.shape[1]
    tb = min(seq, CONV_ROWS)
    nt = seq // tb
    assert seq % tb == 0 and (nt == 1 or tb >= CONF_HIST)
    has_state = hc is not None
    first = (proj.shape[1] - 5 * w) // w
    rows = lambda b, t: b * nt + t
    col = lambda c: pl.BlockSpec((tb, w), lambda b, t: (rows(b, t), first + c))
    vec = lambda r: pl.BlockSpec((r, w), lambda b, t: (0, 0))
    in_specs = [col(0), col(1), col(2), col(3), col(4), vec(CONF_K), vec(1), vec(1), vec(1), vec(SC_K)]
    args = [proj] * 5 + [wc, bc, lg, lb, ws]
    if has_state:
        in_specs += [pl.BlockSpec((1, CONF_K - 1, w), lambda b, t: (b, 0, 0)),
                     pl.BlockSpec((1, SC_K - 1, w), lambda b, t: (b, 0, 0))]
        args += [hc, hs]
    return pl.pallas_call(
        functools.partial(_conv_kernel, tb=tb, has_state=has_state),
        grid=(batch, nt),
        in_specs=in_specs,
        out_specs=[
            pl.BlockSpec((tb, w), lambda b, t: (rows(b, t), 0)),
            pl.BlockSpec((tb, w), lambda b, t: (rows(b, t), 0)),
            pl.BlockSpec((1, CONF_K - 1, w), lambda b, t: (b, 0, 0)),
            pl.BlockSpec((1, SC_K - 1, w), lambda b, t: (b, 0, 0)),
        ],
        out_shape=[jax.ShapeDtypeStruct((n, w), BF16), jax.ShapeDtypeStruct((n, w), BF16),
                   jax.ShapeDtypeStruct((batch, CONF_K - 1, w), F32),
                   jax.ShapeDtypeStruct((batch, SC_K - 1, w), F32)],
        scratch_shapes=[pltpu.VMEM((CONF_HIST + tb, w), F32), pltpu.VMEM((SC_HIST + tb, w), F32)],
        compiler_params=_params(("parallel", "arbitrary"), 32),
        name="convs",
    )(*args)


def _out_proj_kernel(yg_ref, yc_ref, ys_ref, w_ref, h_ref, o_ref):
    v_w = yg_ref.shape[1]
    c_w = yc_ref.shape[1]
    acc = jnp.dot(yg_ref[...], w_ref[0:v_w, :], preferred_element_type=F32)
    acc = acc + jnp.dot(yc_ref[...], w_ref[v_w:v_w + c_w, :], preferred_element_type=F32)
    acc = acc + jnp.dot(ys_ref[...], w_ref[v_w + c_w:, :], preferred_element_type=F32)
    o_ref[...] = h_ref[...] + acc


def _out_proj(yg, yc, ys, w_out, h):
    n, d = h.shape
    bm = min(1024, n)
    bn = 512
    return pl.pallas_call(
        _out_proj_kernel,
        grid=(n // bm, d // bn),
        in_specs=[
            pl.BlockSpec((bm, yg.shape[1]), lambda i, j: (i, 0)),
            pl.BlockSpec((bm, yc.shape[1]), lambda i, j: (i, 0)),
            pl.BlockSpec((bm, ys.shape[1]), lambda i, j: (i, 0)),
            pl.BlockSpec((w_out.shape[0], bn), lambda i, j: (0, j)),
            pl.BlockSpec((bm, bn), lambda i, j: (i, j)),
        ],
        out_specs=pl.BlockSpec((bm, bn), lambda i, j: (i, j)),
        out_shape=jax.ShapeDtypeStruct((n, d), F32),
        compiler_params=_params(("parallel", "arbitrary"), 48),
        name="out_proj",
    )(yg, yc, ys, w_out, h)


def _scores_kernel(h_ref, g_ref, wq_ref, k1_ref, k2_ref, xnt_ref, s1_ref, s2_ref):
    @pl.when(pl.program_id(1) == 0)
    def _():
        def store(rows, xn):
            xnt_ref[:, rows] = xn.T.astype(BF16)

        _rms_row_groups(h_ref, g_ref, store)

    q = jnp.dot(wq_ref[...], xnt_ref[...], preferred_element_type=F32)
    half = q.shape[0] // 2
    s1_ref[0] = jnp.dot(k1_ref[0], q[:half].astype(BF16), preferred_element_type=F32)
    s2_ref[0] = jnp.dot(k2_ref[0], q[half:].astype(BF16), preferred_element_type=F32)


def _scores(h, g, wq_t, k1, k2):
    n, d = h.shape
    bt = min(512, n)
    dq = wq_t.shape[0] // PEER_HEADS
    nk = k1.shape[1]
    return pl.pallas_call(
        _scores_kernel,
        grid=(n // bt, PEER_HEADS),
        in_specs=[
            pl.BlockSpec((bt, d), lambda i, hd: (i, 0)),
            pl.BlockSpec((1, d), lambda i, hd: (0, 0)),
            pl.BlockSpec((dq, d), lambda i, hd: (hd, 0)),
            pl.BlockSpec((1, nk, dq // 2), lambda i, hd: (hd, 0, 0)),
            pl.BlockSpec((1, nk, dq // 2), lambda i, hd: (hd, 0, 0)),
        ],
        out_specs=[
            pl.BlockSpec((d, bt), lambda i, hd: (0, i)),
            pl.BlockSpec((1, nk, bt), lambda i, hd: (hd, 0, i)),
            pl.BlockSpec((1, nk, bt), lambda i, hd: (hd, 0, i)),
        ],
        out_shape=[jax.ShapeDtypeStruct((d, n), BF16),
                   jax.ShapeDtypeStruct((PEER_HEADS, nk, n), F32),
                   jax.ShapeDtypeStruct((PEER_HEADS, nk, n), F32)],
        compiler_params=_params(("parallel", "arbitrary"), 48),
        name="peer_scores",
    )(h, g, wq_t, k1, k2)


def _oddeven_merge_sort_pairs(n):
    pairs = []
    p = 1
    while p < n:
        k = p
        while k >= 1:
            for j in range(k % p, n - k, 2 * k):
                for i in range(min(k, n - j - k)):
                    if (i + j) // (2 * p) == (i + j + k) // (2 * p):
                        pairs.append((i + j, i + j + k))
            k //= 2
        p *= 2
    return pairs


def _bitonic_merge_pairs(n):
    pairs = []
    k = n // 2
    while k >= 1:
        pairs += [(i, i + k) for i in range(n) if (i // k) % 2 == 0]
        k //= 2
    return pairs


_SORT_PAIRS = _oddeven_merge_sort_pairs(PEER_TOPK)
_MERGE_PAIRS = _bitonic_merge_pairs(PEER_TOPK)


def _exchange(v, pairs):
    v = list(v)
    for i, j in pairs:
        v[i], v[j] = jnp.maximum(v[i], v[j]), jnp.minimum(v[i], v[j])
    return v


def _key_tile(a):
    start = a * PEER_HEADS
    return pl.ds(start if isinstance(start, int) else pl.multiple_of(start, PEER_HEADS), PEER_HEADS)


def _top_values(s_ref, s_scr, g_scr, v_scr):
    nh, nk, k = PEER_HEADS, PEER_NKEYS, PEER_TOPK
    n_groups = nk // k
    for grp in range(n_groups):
        vals = []
        for i in range(k):
            a = grp * k + i
            t = s_ref[pl.ds(a, nh, stride=nk), :]
            s_scr[a * nh:(a + 1) * nh, :] = t
            vals.append(t)
        for i, t in enumerate(_exchange(vals, _SORT_PAIRS)):
            g_scr[(grp * k + i) * nh:(grp * k + i + 1) * nh, :] = t
    span = 1
    while span < n_groups:
        for grp in range(0, n_groups, 2 * span):
            lo = [g_scr[(grp * k + i) * nh:(grp * k + i + 1) * nh, :] for i in range(k)]
            hi = [g_scr[((grp + span) * k + i) * nh:((grp + span) * k + i + 1) * nh, :] for i in range(k)]
            top = _exchange([jnp.maximum(lo[i], hi[k - 1 - i]) for i in range(k)], _MERGE_PAIRS)
            for i, t in enumerate(top):
                g_scr[(grp * k + i) * nh:(grp * k + i + 1) * nh, :] = t
        span *= 2
    for i in range(k):
        v_scr[i] = g_scr[i * nh:(i + 1) * nh, :]


def _ties(s_scr, v_scr):
    k = PEER_TOPK
    v = [v_scr[i] for i in range(k)]
    bad = jnp.zeros_like(v[0])
    for i in range(k - 1):
        bad = bad + jnp.where(v[i] == v[i + 1], 1.0, 0.0)

    def count(a, c):
        return c + jnp.where(s_scr[_key_tile(a), :] >= v[k - 1], 1.0, 0.0)

    n_ge = lax.fori_loop(0, PEER_NKEYS, count, jnp.zeros_like(v[0]), unroll=8)
    return bad + jnp.where(n_ge == float(k), 0.0, 1.0)


def _rank_keys(s_scr, r_scr):
    nh, nk = PEER_HEADS, PEER_NKEYS
    bt = s_scr.shape[1]

    def init(a, m):
        r_scr[_key_tile(a), :] = jnp.full((nh, bt), NOT_SELECTED, F32)
        return jnp.maximum(m, s_scr[_key_tile(a), :])

    m0 = lax.fori_loop(0, nk, init, jnp.full((nh, bt), -jnp.inf, F32), unroll=8)

    def round_body(r, m):
        def find(a, idx):
            return jnp.minimum(idx, jnp.where(s_scr[_key_tile(a), :] == m, lax.convert_element_type(a, F32),
                                              float(nk)))

        idx = lax.fori_loop(0, nk, find, jnp.full((nh, bt), float(nk), F32), unroll=8)
        rank = lax.convert_element_type(r, F32)

        def knock(a, m_next):
            hit = idx == lax.convert_element_type(a, F32)
            s_new = jnp.where(hit, -jnp.inf, s_scr[_key_tile(a), :])
            s_scr[_key_tile(a), :] = s_new
            r_scr[_key_tile(a), :] = jnp.where(hit, rank, r_scr[_key_tile(a), :])
            return jnp.maximum(m_next, s_new)

        return lax.fori_loop(0, nk, knock, jnp.full((nh, bt), -jnp.inf, F32), unroll=8)

    lax.fori_loop(0, PEER_TOPK, round_body, m0)


def _topk_kernel(s1_ref, s2_ref, c1_ref, e1_ref, r2_ref, e2_ref,
                 s1_scr, s2_scr, g_scr, r2_scr, e2k_scr, v1_scr, v2_scr, c_scr, e1z_scr, e2_scr, t_scr):
    nh, nk, k = PEER_HEADS, PEER_NKEYS, PEER_TOPK
    _top_values(s1_ref, s1_scr, g_scr, v1_scr)
    _top_values(s2_ref, s2_scr, g_scr, v2_scr)

    v1 = [v1_scr[i] for i in range(PEER_TOPK)]
    v2 = [v2_scr[j] for j in range(PEER_TOPK)]
    cand = {(i, j): v1[i] + v2[j] for (i, j) in _CELLS}
    before = {p: 0.0 for p in _CELLS}
    for x, p in enumerate(_CELLS):
        for q in _CELLS[x + 1:]:
            if p[0] <= q[0] and p[1] <= q[1]:
                before[q] = before[q] + 1.0
            elif not (q[0] <= p[0] and q[1] <= p[1]):
                p_first = jnp.where(cand[p] >= cand[q], 1.0, 0.0)
                before[q] = before[q] + p_first
                before[p] = before[p] + (1.0 - p_first)
    e1 = [jnp.exp(v1[i] - v1[0]) for i in range(PEER_TOPK)]
    e2 = [jnp.exp(v2[j] - v2[0]) for j in range(PEER_TOPK)]
    height = [jnp.zeros_like(v1[0]) for _ in range(PEER_TOPK)]
    z = jnp.zeros_like(v1[0])
    for (i, j) in _CELLS:
        sel = jnp.where(before[(i, j)] < float(PEER_TOPK), 1.0, 0.0)
        height[i] = height[i] + sel
        z = z + sel * (e1[i] * e2[j])
    inv_z = 1.0 / z
    for i in range(PEER_TOPK):
        c_scr[i] = height[i]
        e1z_scr[i] = e1[i] * inv_z
        e2_scr[i] = e2[i]
    for c in range(1, k + 1):
        t = jnp.full_like(v1[0], jnp.inf)
        for i in range(k):
            t = jnp.minimum(t, jnp.where(height[i] >= float(c), v1[i], jnp.inf))
        t_scr[c - 1] = t

    n_ties = jnp.sum(_ties(s1_scr, v1_scr) + _ties(s2_scr, v2_scr))

    @pl.when(n_ties == 0.0)
    def _():
        v1_max = v1_scr[0]
        v2_max = v2_scr[0]
        inv = e1z_scr[0]

        def emit1(a, carry):
            s = s1_scr[_key_tile(a), :]
            c = jnp.zeros_like(s)
            for lvl in range(k):
                c = c + jnp.where(s >= t_scr[lvl], 1.0, 0.0)
            c1_ref[a] = c
            e1_ref[a] = jnp.exp(s - v1_max) * inv
            return carry

        lax.fori_loop(0, nk, emit1, 0, unroll=4)

        def emit2(b, carry):
            s = s2_scr[_key_tile(b), :]
            r = jnp.zeros_like(s)
            for i in range(k):
                r = r + jnp.where(v2_scr[i] > s, 1.0, 0.0)
            r2_scr[_key_tile(b), :] = r
            e2k_scr[_key_tile(b), :] = jnp.exp(s - v2_max)
            return carry

        lax.fori_loop(0, nk, emit2, 0, unroll=4)

    @pl.when(n_ties != 0.0)
    def _():
        _rank_keys(s1_scr, g_scr)

        def emit1(a, carry):
            rk = g_scr[_key_tile(a), :]
            c = jnp.zeros_like(rk)
            e = jnp.zeros_like(rk)
            for r in range(k):
                eq = rk == float(r)
                c = jnp.where(eq, c_scr[r], c)
                e = jnp.where(eq, e1z_scr[r], e)
            c1_ref[a] = c
            e1_ref[a] = e
            return carry

        lax.fori_loop(0, nk, emit1, 0, unroll=4)
        _rank_keys(s2_scr, r2_scr)

        def emit2(b, carry):
            rk = r2_scr[_key_tile(b), :]
            e = jnp.zeros_like(rk)
            for r in range(k):
                e = jnp.where(rk == float(r), e2_scr[r], e)
            e2k_scr[_key_tile(b), :] = e
            return carry

        lax.fori_loop(0, nk, emit2, 0, unroll=4)

    for hd in range(nh):
        r2_ref[hd] = r2_scr[pl.ds(hd, nk, stride=nh), :]
        e2_ref[hd] = e2k_scr[pl.ds(hd, nk, stride=nh), :]


def _topk(s1, s2):
    nh, nk, n = s1.shape
    bt = LANES
    rows = nh * nk
    big = pltpu.VMEM((rows, bt), F32)
    small = pltpu.VMEM((PEER_TOPK, nh, bt), F32)
    return pl.pallas_call(
        _topk_kernel,
        grid=(n // bt,),
        in_specs=[pl.BlockSpec((rows, bt), lambda i: (0, i)), pl.BlockSpec((rows, bt), lambda i: (0, i))],
        out_specs=[
            pl.BlockSpec((nk, nh, bt), lambda i: (0, 0, i)),
            pl.BlockSpec((nk, nh, bt), lambda i: (0, 0, i)),
            pl.BlockSpec((nh, nk, bt), lambda i: (0, 0, i)),
            pl.BlockSpec((nh, nk, bt), lambda i: (0, 0, i)),
        ],
        out_shape=[jax.ShapeDtypeStruct((nk, nh, n), F32), jax.ShapeDtypeStruct((nk, nh, n), F32),
                   jax.ShapeDtypeStruct((nh, nk, n), F32), jax.ShapeDtypeStruct((nh, nk, n), F32)],
        scratch_shapes=[big] * 5 + [small] * 6,
        compiler_params=_params(("parallel",), 40),
        name="peer_topk",
    )(s1.reshape(rows, n), s2.reshape(rows, n))


def _peer_ffn_kernel(xnt_ref, u_ref, v_ref, c1_ref, e1_ref, r2_ref, e2_ref, out_ref, w_scr):
    nk = PEER_NKEYS
    j = pl.program_id(1)
    last = pl.num_programs(1) - 1
    bt = out_ref.shape[1]

    def accumulate():
        out_ref[...] += lax.dot_general(v_ref[...], w_scr[(j + 1) % 2], (((0,), (0,)), ((), ())),
                                        preferred_element_type=F32)

    def weigh():
        at = jnp.dot(u_ref[...], xnt_ref[...], preferred_element_type=F32)
        cur = j % 2
        for al in range(A_PER_BLOCK):
            rows = slice(al * nk, (al + 1) * nk)
            for l0 in range(0, bt, LANES):
                lanes = slice(l0, l0 + LANES)
                g = jnp.zeros((nk, LANES), F32)
                for hd in range(PEER_HEADS):
                    sel = r2_ref[hd, :, lanes] < c1_ref[al, hd:hd + 1, lanes]
                    g = g + jnp.where(sel, e1_ref[al, hd:hd + 1, lanes] * e2_ref[hd, :, lanes], 0.0)
                w_scr[cur, rows, lanes] = (g * jax.nn.gelu(at[rows, lanes])).astype(BF16)

    @pl.when(j == 0)
    def _():
        out_ref[...] = jnp.zeros_like(out_ref)
        weigh()

    @pl.when(jnp.logical_and(j > 0, j < last))
    def _():
        accumulate()
        weigh()

    @pl.when(j == last)
    def _():
        accumulate()


def _peer_ffn(xnt, u, v, c1, e1, r2, e2):
    d, n = xnt.shape
    n_exp = u.shape[0]
    nk, nh, _ = c1.shape
    bt = min(512, n)
    be = A_PER_BLOCK * nk
    nj = n_exp // be
    once = pl.Buffered(1)
    return pl.pallas_call(
        _peer_ffn_kernel,
        grid=(n // bt, nj + 1),
        in_specs=[
            pl.BlockSpec((d, bt), lambda i, j: (0, i), pipeline_mode=once),
            pl.BlockSpec((be, d), lambda i, j: (jnp.minimum(j, nj - 1), 0)),
            pl.BlockSpec((be, d), lambda i, j: (jnp.maximum(j - 1, 0), 0)),
            pl.BlockSpec((A_PER_BLOCK, nh, bt), lambda i, j: (jnp.minimum(j, nj - 1), 0, i)),
            pl.BlockSpec((A_PER_BLOCK, nh, bt), lambda i, j: (jnp.minimum(j, nj - 1), 0, i)),
            pl.BlockSpec((nh, nk, bt), lambda i, j: (0, 0, i), pipeline_mode=once),
            pl.BlockSpec((nh, nk, bt), lambda i, j: (0, 0, i), pipeline_mode=once),
        ],
        out_specs=pl.BlockSpec((d, bt), lambda i, j: (0, i)),
        out_shape=jax.ShapeDtypeStruct((d, n), F32),
        scratch_shapes=[pltpu.VMEM((2, be, bt), BF16)],
        compiler_params=_params(("parallel", "arbitrary"), 56),
        name="peer_ffn",
    )(xnt, u, v, c1, e1, r2, e2)


def _add_t_kernel(h_ref, ot_ref, o_ref):
    o_ref[...] = h_ref[...] + ot_ref[...].T


def _add_transposed(h, out_t):
    n, d = h.shape
    bm = min(256, n)
    return pl.pallas_call(
        _add_t_kernel,
        grid=(n // bm,),
        in_specs=[pl.BlockSpec((bm, d), lambda i: (i, 0)), pl.BlockSpec((d, bm), lambda i: (0, i))],
        out_specs=pl.BlockSpec((bm, d), lambda i: (i, 0)),
        out_shape=jax.ShapeDtypeStruct((n, d), F32),
        compiler_params=_params(("parallel",), 56),
        name="add_transposed",
    )(h, out_t)


def _ple_kernel(h_ref, g_ref, wg_ref, p_ref, wp_ref, hcol_ref, o_ref, xn_ref):
    @pl.when(pl.program_id(1) == 0)
    def _():
        def store(rows, xn):
            xn_ref[rows, :] = xn.astype(BF16)

        _rms_row_groups(h_ref, g_ref, store)

    gate = jax.nn.sigmoid(jnp.dot(xn_ref[...], wg_ref[...], preferred_element_type=F32))
    pp = jnp.dot(p_ref[...], wp_ref[...], preferred_element_type=F32)
    o_ref[...] = hcol_ref[...] + pp * gate


def _ple(h, g, w_gate, p, w_proj):
    n, d = h.shape
    bm = min(1024, n)
    bn = 512
    pd = p.shape[1]
    return pl.pallas_call(
        _ple_kernel,
        grid=(n // bm, d // bn),
        in_specs=[
            pl.BlockSpec((bm, d), lambda i, j: (i, 0), pipeline_mode=pl.Buffered(1)),
            pl.BlockSpec((1, d), lambda i, j: (0, 0)),
            pl.BlockSpec((d, bn), lambda i, j: (0, j)),
            pl.BlockSpec((bm, pd), lambda i, j: (i, 0)),
            pl.BlockSpec((pd, bn), lambda i, j: (0, j)),
            pl.BlockSpec((bm, bn), lambda i, j: (i, j)),
        ],
        out_specs=pl.BlockSpec((bm, bn), lambda i, j: (i, j)),
        out_shape=jax.ShapeDtypeStruct((n, d), F32),
        scratch_shapes=[pltpu.VMEM((bm, d), BF16)],
        compiler_params=_params(("parallel", "arbitrary"), 56),
        name="ple",
    )(h, g, w_gate, p, w_proj, h)


def _final_norm_kernel(h_ref, g_ref, o_ref):
    def store(rows, xn):
        o_ref[rows, :] = xn

    _rms_row_groups(h_ref, g_ref, store)


def _final_norm(h, g):
    n, d = h.shape
    bm = min(512, n)
    return pl.pallas_call(
        _final_norm_kernel,
        grid=(n // bm,),
        in_specs=[pl.BlockSpec((bm, d), lambda i: (i, 0)), pl.BlockSpec((1, d), lambda i: (0, 0))],
        out_specs=pl.BlockSpec((bm, d), lambda i: (i, 0)),
        out_shape=jax.ShapeDtypeStruct((n, d), F32),
        compiler_params=_params(("parallel",), 48),
        name="final_norm",
    )(h, g)


def _cast_kernel(x_ref, o_ref):
    o_ref[...] = x_ref[...].astype(o_ref.dtype)


def _cast_layer(x, layer):
    _, r, c = x.shape
    bm = min(CAST_ROWS, r)
    return pl.pallas_call(
        _cast_kernel,
        grid=(r // bm,),
        in_specs=[pl.BlockSpec((None, bm, c), lambda i: (layer, i, 0))],
        out_specs=pl.BlockSpec((bm, c), lambda i: (i, 0)),
        out_shape=jax.ShapeDtypeStruct((r, c), BF16),
        compiler_params=_params(("parallel",), 40),
        name="cast_weight",
    )(x)


def _w_in_t_kernel(a_ref, b_ref, o_ref, *, n_lo):
    k = pl.program_id(0)
    bm = o_ref.shape[0]

    @pl.when(k < n_lo)
    def _():
        o_ref[...] = a_ref[...].astype(BF16)

    @pl.when(k >= n_lo)
    def _():
        o_ref[0:bm - GLA_LR, :] = a_ref[GLA_LR:bm, :].astype(BF16)
        o_ref[bm - GLA_LR:bm, :] = b_ref[...].astype(BF16)


def _split_w_in_t(w_in_t, layer):
    _, cols, d = w_in_t.shape
    lr0 = 2 * (d // 4) + 2 * (d // 2)
    bm = CAST_ROWS
    n_out = cols - GLA_LR
    assert lr0 % bm == 0 and n_out % bm == 0 and bm % GLA_LR == 0
    return pl.pallas_call(
        functools.partial(_w_in_t_kernel, n_lo=lr0 // bm),
        grid=(n_out // bm,),
        in_specs=[pl.BlockSpec((None, bm, d), lambda k: (layer, k, 0)),
                  pl.BlockSpec((None, GLA_LR, d), lambda k: (layer, (k + 1) * (bm // GLA_LR), 0))],
        out_specs=pl.BlockSpec((bm, d), lambda k: (k, 0)),
        out_shape=jax.ShapeDtypeStruct((n_out, d), BF16),
        compiler_params=_params(("parallel",), 40),
        name="split_w_in",
    )(w_in_t, w_in_t)


def _prep_layer(i, g_mix, w_in, gla_w_lr, gla_b_lr, gla_g_out, conf_w_dw, conf_b_dw, conf_ln_g, conf_ln_b,
                sc_w_dw, w_out, g_ffn, peer_w_q, peer_k1, peer_k2, peer_u, peer_v, g_ple, ple_w_gate, ple_w_proj):
    row = lambda x: x[i][None, :]
    d = w_in.shape[1]
    lr0 = 2 * (d // 4) + 2 * (d // 2)
    w_in_t = jnp.swapaxes(w_in, 1, 2)
    return dict(
        g_mix=row(g_mix), w_main=_split_w_in_t(w_in_t, i),
        w_lr_in=jnp.pad(w_in_t[i, lr0:lr0 + GLA_LR], ((0, LANES - GLA_LR), (0, 0))).astype(BF16),
        w_lr=jnp.pad(gla_w_lr[i], ((0, LANES - GLA_LR), (0, 0))).astype(BF16),
        b_lr=row(gla_b_lr), gg=row(gla_g_out),
        wc=conf_w_dw[i], bc=row(conf_b_dw), lg=row(conf_ln_g), lb=row(conf_ln_b), ws=sc_w_dw[i],
        w_out=_cast_layer(w_out, i),
        g_ffn=row(g_ffn), wq_t=peer_w_q[i].T.astype(BF16),
        k1=peer_k1[i].astype(BF16), k2=peer_k2[i].astype(BF16),
        u=_cast_layer(peer_u, i), v=_cast_layer(peer_v, i),
        g_ple=row(g_ple), w_gate=_cast_layer(ple_w_gate, i), w_proj=ple_w_proj[i].astype(BF16),
    )


def _layer(h, p, s_gla, s_conf, s_sc, lw, batch, seq):
    proj, la = _in_proj(h, lw["g_mix"], lw["w_main"], lw["w_lr_in"], lw["w_lr"], lw["b_lr"])
    y_gla, new_gla = _gla(proj, la, lw["gg"], s_gla, batch, seq)
    y_conf, y_sc, new_conf, new_sc = _convs(proj, s_conf, s_sc, lw["wc"], lw["bc"], lw["lg"], lw["lb"], lw["ws"],
                                            batch, seq)
    h = _out_proj(y_gla, y_conf, y_sc, lw["w_out"], h)
    xnt, s1, s2 = _scores(h, lw["g_ffn"], lw["wq_t"], lw["k1"], lw["k2"])
    c1, e1, r2, e2 = _topk(s1, s2)
    h = _add_transposed(h, _peer_ffn(xnt, lw["u"], lw["v"], c1, e1, r2, e2))
    h = _ple(h, lw["g_ple"], lw["w_gate"], p.astype(BF16), lw["w_proj"])
    return h, new_gla, new_conf, new_sc


def _trunk(x, p, s_gla, s_conf, s_sc, layers, g_final):
    batch, seq, d = x.shape
    h = x.reshape(batch * seq, d)
    new_gla, new_conf, new_sc = [], [], []
    for i, lw in enumerate(layers):
        st = (None, None, None) if s_gla is None else (s_gla[i], s_conf[i], s_sc[i])
        h, sg, scf, ssc = _layer(h, p[i].reshape(batch * seq, -1), *st, lw, batch, seq)
        new_gla.append(sg)
        new_conf.append(scf)
        new_sc.append(ssc)
    y = _final_norm(h, g_final[None, :]).reshape(batch, seq, d)
    return y, jnp.stack(new_gla), jnp.stack(new_conf), jnp.stack(new_sc)


def kernel(x_prompt, x_sample, state_gla, state_conf, state_sconv, p_prompt, p_sample, g_mix, w_in, gla_w_lr, gla_b_lr, gla_g_out, conf_w_dw, conf_b_dw, conf_ln_g, conf_ln_b, sc_w_dw, w_out, g_ffn, peer_w_q, peer_k1, peer_k2, peer_u, peer_v, g_ple, ple_w_gate, ple_w_proj, g_final):
    layers = [_prep_layer(i, g_mix, w_in, gla_w_lr, gla_b_lr, gla_g_out, conf_w_dw, conf_b_dw, conf_ln_g,
                          conf_ln_b, sc_w_dw, w_out, g_ffn, peer_w_q, peer_k1, peer_k2, peer_u, peer_v, g_ple,
                          ple_w_gate, ple_w_proj) for i in range(w_in.shape[0])]
    y_p, gla_p, conf_p, sc_p = _trunk(x_prompt, p_prompt, None, None, None, layers, g_final)
    y_s, gla_s, conf_s, sc_s = _trunk(x_sample, p_sample, state_gla, state_conf, state_sconv, layers, g_final)
    return (y_p, y_s, gla_p, conf_p, sc_p, gla_s, conf_s, sc_s)
```

```python
import functools

import jax
import jax.numpy as jnp
from jax import lax
from jax.experimental import pallas as pl
from jax.experimental.pallas import tpu as pltpu

F32 = jnp.float32
BF16 = jnp.bfloat16

EPS = 1e-6
CHUNK = 64
GLA_HEADS = 4
GLA_LR = 16
GLA_TAU = 16.0
CONF_K = 31
SC_K = 3
PEER_HEADS = 8
PEER_NKEYS = 128
PEER_TOPK = 16

LANES = 128
SUBLANES = 8
CONF_HIST = 32
SC_HIST = 8
NORM_ROWS = 128
CAST_ROWS = 512
GLA_ROWS = 256
CONV_ROWS = 256
A_PER_BLOCK = 4
NOT_SELECTED = 64.0
MIB = 1024 * 1024

_CELLS = [(i, j) for i in range(PEER_TOPK) for j in range(PEER_TOPK) if (i + 1) * (j + 1) <= PEER_TOPK]


def _params(semantics, vmem_mib, flags=None):
    return pltpu.CompilerParams(dimension_semantics=semantics, vmem_limit_bytes=vmem_mib * MIB, flags=flags)


def _rms(x, g):
    ms = jnp.mean(x * x, axis=-1, keepdims=True)
    return x * lax.rsqrt(ms + EPS) * g


def _rms_row_groups(h_ref, g_ref, store):
    g = g_ref[...]
    for r0 in range(0, h_ref.shape[0], NORM_ROWS):
        rows = slice(r0, min(r0 + NORM_ROWS, h_ref.shape[0]))
        store(rows, _rms(h_ref[rows, :], g))


def _in_proj_kernel(h_ref, g_ref, w_ref, wlr_in_ref, wlr_ref, blr_ref, proj_ref, la_ref, xn_ref):
    nt = (((1,), (1,)), ((), ()))

    @pl.when(pl.program_id(1) == 0)
    def _():
        def store(rows, xn):
            xn = xn.astype(BF16)
            xn_ref[rows, :] = xn
            lr = lax.dot_general(xn, wlr_in_ref[...], nt, preferred_element_type=F32)
            z = jnp.dot(lr.astype(BF16), wlr_ref[...], preferred_element_type=F32) + blr_ref[...]
            la_ref[rows, :] = (jnp.minimum(z, 0.0) - jnp.log1p(jnp.exp(-jnp.abs(z)))) * (1.0 / GLA_TAU)

        _rms_row_groups(h_ref, g_ref, store)

    proj_ref[...] = lax.dot_general(xn_ref[...], w_ref[...], nt, preferred_element_type=F32)


def _in_proj(h, g, w_main, w_lr_in, w_lr, b_lr):
    n, d = h.shape
    ncols = w_main.shape[0]
    qk = w_lr.shape[1]
    bm = min(1024, n)
    bn = d // 8
    return pl.pallas_call(
        _in_proj_kernel,
        grid=(n // bm, ncols // bn),
        in_specs=[
            pl.BlockSpec((bm, d), lambda i, j: (i, 0), pipeline_mode=pl.Buffered(1)),
            pl.BlockSpec((1, d), lambda i, j: (0, 0)),
            pl.BlockSpec((bn, d), lambda i, j: (j, 0)),
            pl.BlockSpec((LANES, d), lambda i, j: (0, 0)),
            pl.BlockSpec((LANES, qk), lambda i, j: (0, 0)),
            pl.BlockSpec((1, qk), lambda i, j: (0, 0)),
        ],
        out_specs=[
            pl.BlockSpec((bm, bn), lambda i, j: (i, j)),
            pl.BlockSpec((bm, qk), lambda i, j: (i, 0)),
        ],
        out_shape=[jax.ShapeDtypeStruct((n, ncols), F32), jax.ShapeDtypeStruct((n, qk), F32)],
        scratch_shapes=[pltpu.VMEM((bm, d), BF16)],
        compiler_params=_params(("parallel", "arbitrary"), 56),
        name="in_proj",
    )(h, g, w_main, w_lr_in, w_lr, b_lr)


def _gla_kernel(*refs, chunk, n_chunks, has_state, scale):
    if has_state:
        q_ref, k_ref, v_ref, g_ref, la_ref, gg_ref, s0_ref, y_ref, sout_ref, st_ref = refs
    else:
        q_ref, k_ref, v_ref, g_ref, la_ref, gg_ref, y_ref, sout_ref, st_ref = refs
    tt = pl.program_id(1)
    n_heads, dv, dk = st_ref.shape

    @pl.when(tt == 0)
    def _():
        for hd in range(n_heads):
            st_ref[hd] = s0_ref[0, hd].T if has_state else jnp.zeros((dv, dk), F32)

    row = lax.broadcasted_iota(jnp.int32, (chunk, chunk), 0)
    col = lax.broadcasted_iota(jnp.int32, (chunk, chunk), 1)
    causal = row >= col
    tril = jnp.where(causal, 1.0, 0.0).astype(BF16)
    mid = (chunk - 1) // 2
    nt = (((1,), (1,)), ((), ()))
    tn = (((0,), (0,)), ((), ()))

    for c in range(n_chunks):
        sl = pl.ds(c * chunk, chunk)
        la = la_ref[sl, :]
        la_hi = la.astype(BF16)
        r1 = la - la_hi.astype(F32)
        la_mid = r1.astype(BF16)
        la_lo = (r1 - la_mid.astype(F32)).astype(BF16)
        cb_all = (jnp.dot(tril, la_hi, preferred_element_type=F32)
                  + jnp.dot(tril, la_mid, preferred_element_type=F32)
                  + jnp.dot(tril, la_lo, preferred_element_type=F32))
        for hd in range(n_heads):
            kc = slice(hd * dk, (hd + 1) * dk)
            vc = slice(hd * dv, (hd + 1) * dv)
            cb = cb_all[:, kc]
            c_last = cb[chunk - 1:chunk, :]
            c_mid = cb[mid:mid + 1, :]
            q = q_ref[sl, kc] * scale
            k = k_ref[sl, kc]
            vb = v_ref[sl, vc].astype(BF16)
            st = st_ref[hd]
            qe = (q * jnp.exp(cb - c_mid)).astype(BF16)
            ke = (k * jnp.exp(c_mid - cb)).astype(BF16)
            att = lax.dot_general(qe, ke, nt, preferred_element_type=F32)
            att = jnp.where(causal, att, 0.0).astype(BF16)
            q2 = (q * jnp.exp(cb)).astype(BF16)
            o = (jnp.dot(att, vb, preferred_element_type=F32)
                 + lax.dot_general(q2, st.astype(BF16), nt, preferred_element_type=F32))
            k2 = (k * jnp.exp(c_last - cb)).astype(BF16)
            st_ref[hd] = jnp.exp(c_last) * st + lax.dot_general(vb, k2, tn, preferred_element_type=F32)
            o = o * lax.rsqrt(jnp.mean(o * o, axis=-1, keepdims=True) + EPS)
            gate = g_ref[sl, vc]
            y_ref[sl, vc] = (o * gg_ref[:, vc] * (gate * jax.nn.sigmoid(gate))).astype(y_ref.dtype)

    @pl.when(tt == pl.num_programs(1) - 1)
    def _():
        for hd in range(n_heads):
            sout_ref[0, hd] = st_ref[hd].T


def _gla(proj, la, gg, s0, batch, seq):
    n = proj.shape[0]
    qk = la.shape[1]
    v_w = gg.shape[1]
    dk = qk // GLA_HEADS
    dv = v_w // GLA_HEADS
    chunk = min(CHUNK, seq)
    tb = min(seq, GLA_ROWS)
    assert seq % tb == 0 and tb % chunk == 0 and 2 * qk % v_w == 0
    nt = seq // tb
    has_state = s0 is not None
    rows = lambda b, t: b * nt + t
    v_blk = 2 * qk // v_w
    in_specs = [
        pl.BlockSpec((tb, qk), lambda b, t: (rows(b, t), 0)),
        pl.BlockSpec((tb, qk), lambda b, t: (rows(b, t), 1)),
        pl.BlockSpec((tb, v_w), lambda b, t: (rows(b, t), v_blk)),
        pl.BlockSpec((tb, v_w), lambda b, t: (rows(b, t), v_blk + 1)),
        pl.BlockSpec((tb, qk), lambda b, t: (rows(b, t), 0)),
        pl.BlockSpec((1, v_w), lambda b, t: (0, 0)),
    ]
    args = [proj, proj, proj, proj, la, gg]
    if has_state:
        in_specs.append(pl.BlockSpec((1, GLA_HEADS, dk, dv), lambda b, t: (b, 0, 0, 0)))
        args.append(s0)
    return pl.pallas_call(
        functools.partial(_gla_kernel, chunk=chunk, n_chunks=tb // chunk, has_state=has_state, scale=dk ** -0.5),
        grid=(batch, nt),
        in_specs=in_specs,
        out_specs=[
            pl.BlockSpec((tb, v_w), lambda b, t: (rows(b, t), 0)),
            pl.BlockSpec((1, GLA_HEADS, dk, dv), lambda b, t: (b, 0, 0, 0)),
        ],
        out_shape=[jax.ShapeDtypeStruct((n, v_w), BF16),
                   jax.ShapeDtypeStruct((batch, GLA_HEADS, dk, dv), F32)],
        scratch_shapes=[pltpu.VMEM((GLA_HEADS, dv, dk), F32)],
        compiler_params=_params(("parallel", "arbitrary"), 48),
        name="gla",
    )(*args)


def _conv_kernel(*refs, tb, has_state):
    if has_state:
        (ca_ref, cg_ref, sb_ref, scg_ref, sh_ref, wc_ref, bc_ref, lg_ref, lb_ref, ws_ref, hc_ref, hs_ref,
         yc_ref, ys_ref, nc_ref, ns_ref, ubuf, zbuf) = refs
    else:
        (ca_ref, cg_ref, sb_ref, scg_ref, sh_ref, wc_ref, bc_ref, lg_ref, lb_ref, ws_ref,
         yc_ref, ys_ref, nc_ref, ns_ref, ubuf, zbuf) = refs
    tt = pl.program_id(1)
    c_skip = CONF_HIST - (CONF_K - 1)
    s_skip = SC_HIST - (SC_K - 1)

    @pl.when(tt == 0)
    def _():
        ubuf[0:CONF_HIST, :] = jnp.zeros((CONF_HIST, ubuf.shape[1]), F32)
        zbuf[0:SC_HIST, :] = jnp.zeros((SC_HIST, zbuf.shape[1]), F32)
        if has_state:
            ubuf[c_skip:CONF_HIST, :] = hc_ref[0]
            zbuf[s_skip:SC_HIST, :] = hs_ref[0]

    @pl.when(tt > 0)
    def _():
        ubuf[0:CONF_HIST, :] = ubuf[tb:tb + CONF_HIST, :]
        zbuf[0:SC_HIST, :] = zbuf[tb:tb + SC_HIST, :]

    ubuf[CONF_HIST:CONF_HIST + tb, :] = ca_ref[...] * jax.nn.sigmoid(cg_ref[...])
    zbuf[SC_HIST:SC_HIST + tb, :] = scg_ref[...] * sh_ref[...]

    acc = wc_ref[0:1, :] * ubuf[c_skip:c_skip + tb, :]
    for j in range(1, CONF_K):
        acc = acc + wc_ref[j:j + 1, :] * ubuf[c_skip + j:c_skip + j + tb, :]
    c = acc + bc_ref[...]
    mu = jnp.mean(c, axis=-1, keepdims=True)
    cc = c - mu
    var = jnp.mean(cc * cc, axis=-1, keepdims=True)
    y = cc * lax.rsqrt(var + EPS) * lg_ref[...] + lb_ref[...]
    yc_ref[...] = (y * jax.nn.sigmoid(y)).astype(yc_ref.dtype)

    z = ws_ref[0:1, :] * zbuf[s_skip:s_skip + tb, :]
    for j in range(1, SC_K):
        z = z + ws_ref[j:j + 1, :] * zbuf[s_skip + j:s_skip + j + tb, :]
    ys_ref[...] = (sb_ref[...] * z).astype(ys_ref.dtype)

    @pl.when(tt == pl.num_programs(1) - 1)
    def _():
        nc_ref[0] = ubuf[tb + c_skip:tb + CONF_HIST, :]
        ns_ref[0] = zbuf[tb + s_skip:tb + SC_HIST, :]


def _convs(proj, hc, hs, wc, bc, lg, lb, ws, batch, seq):
    n = proj.shape[0]
    w = wc.shape[1]
    tb = min(seq, CONV_ROWS)
    nt = seq // tb
    assert seq % tb == 0 and (nt == 1 or tb >= CONF_HIST)
    has_state = hc is not None
    first = (proj.shape[1] - 5 * w) // w
    rows = lambda b, t: b * nt + t
    col = lambda c: pl.BlockSpec((tb, w), lambda b, t: (rows(b, t), first + c))
    vec = lambda r: pl.BlockSpec((r, w), lambda b, t: (0, 0))
    in_specs = [col(0), col(1), col(2), col(3), col(4), vec(CONF_K), vec(1), vec(1), vec(1), vec(SC_K)]
    args = [proj] * 5 + [wc, bc, lg, lb, ws]
    if has_state:
        in_specs += [pl.BlockSpec((1, CONF_K - 1, w), lambda b, t: (b, 0, 0)),
                     pl.BlockSpec((1, SC_K - 1, w), lambda b, t: (b, 0, 0))]
        args += [hc, hs]
    return pl.pallas_call(
        functools.partial(_conv_kernel, tb=tb, has_state=has_state),
        grid=(batch, nt),
        in_specs=in_specs,
        out_specs=[
            pl.BlockSpec((tb, w), lambda b, t: (rows(b, t), 0)),
            pl.BlockSpec((tb, w), lambda b, t: (rows(b, t), 0)),
            pl.BlockSpec((1, CONF_K - 1, w), lambda b, t: (b, 0, 0)),
            pl.BlockSpec((1, SC_K - 1, w), lambda b, t: (b, 0, 0)),
        ],
        out_shape=[jax.ShapeDtypeStruct((n, w), BF16), jax.ShapeDtypeStruct((n, w), BF16),
                   jax.ShapeDtypeStruct((batch, CONF_K - 1, w), F32),
                   jax.ShapeDtypeStruct((batch, SC_K - 1, w), F32)],
        scratch_shapes=[pltpu.VMEM((CONF_HIST + tb, w), F32), pltpu.VMEM((SC_HIST + tb, w), F32)],
        compiler_params=_params(("parallel", "arbitrary"), 32),
        name="convs",
    )(*args)


def _out_proj_kernel(yg_ref, yc_ref, ys_ref, w_ref, h_ref, o_ref):
    v_w = yg_ref.shape[1]
    c_w = yc_ref.shape[1]
    acc = jnp.dot(yg_ref[...], w_ref[0:v_w, :], preferred_element_type=F32)
    acc = acc + jnp.dot(yc_ref[...], w_ref[v_w:v_w + c_w, :], preferred_element_type=F32)
    acc = acc + jnp.dot(ys_ref[...], w_ref[v_w + c_w:, :], preferred_element_type=F32)
    o_ref[...] = h_ref[...] + acc


def _out_proj(yg, yc, ys, w_out, h):
    n, d = h.shape
    bm = min(1024, n)
    bn = 512
    return pl.pallas_call(
        _out_proj_kernel,
        grid=(n // bm, d // bn),
        in_specs=[
            pl.BlockSpec((bm, yg.shape[1]), lambda i, j: (i, 0)),
            pl.BlockSpec((bm, yc.shape[1]), lambda i, j: (i, 0)),
            pl.BlockSpec((bm, ys.shape[1]), lambda i, j: (i, 0)),
            pl.BlockSpec((w_out.shape[0], bn), lambda i, j: (0, j)),
            pl.BlockSpec((bm, bn), lambda i, j: (i, j)),
        ],
        out_specs=pl.BlockSpec((bm, bn), lambda i, j: (i, j)),
        out_shape=jax.ShapeDtypeStruct((n, d), F32),
        compiler_params=_params(("parallel", "arbitrary"), 48),
        name="out_proj",
    )(yg, yc, ys, w_out, h)


def _scores_kernel(h_ref, g_ref, wq_ref, k1_ref, k2_ref, xnt_ref, s1_ref, s2_ref):
    @pl.when(pl.program_id(1) == 0)
    def _():
        def store(rows, xn):
            xnt_ref[:, rows] = xn.T.astype(BF16)

        _rms_row_groups(h_ref, g_ref, store)

    q = jnp.dot(wq_ref[...], xnt_ref[...], preferred_element_type=F32)
    half = q.shape[0] // 2
    s1_ref[0] = jnp.dot(k1_ref[0], q[:half].astype(BF16), preferred_element_type=F32)
    s2_ref[0] = jnp.dot(k2_ref[0], q[half:].astype(BF16), preferred_element_type=F32)


def _scores(h, g, wq_t, k1, k2):
    n, d = h.shape
    bt = min(512, n)
    dq = wq_t.shape[0] // PEER_HEADS
    nk = k1.shape[1]
    return pl.pallas_call(
        _scores_kernel,
        grid=(n // bt, PEER_HEADS),
        in_specs=[
            pl.BlockSpec((bt, d), lambda i, hd: (i, 0)),
            pl.BlockSpec((1, d), lambda i, hd: (0, 0)),
            pl.BlockSpec((dq, d), lambda i, hd: (hd, 0)),
            pl.BlockSpec((1, nk, dq // 2), lambda i, hd: (hd, 0, 0)),
            pl.BlockSpec((1, nk, dq // 2), lambda i, hd: (hd, 0, 0)),
        ],
        out_specs=[
            pl.BlockSpec((d, bt), lambda i, hd: (0, i)),
            pl.BlockSpec((1, nk, bt), lambda i, hd: (hd, 0, i)),
            pl.BlockSpec((1, nk, bt), lambda i, hd: (hd, 0, i)),
        ],
        out_shape=[jax.ShapeDtypeStruct((d, n), BF16),
                   jax.ShapeDtypeStruct((PEER_HEADS, nk, n), F32),
                   jax.ShapeDtypeStruct((PEER_HEADS, nk, n), F32)],
        compiler_params=_params(("parallel", "arbitrary"), 48),
        name="peer_scores",
    )(h, g, wq_t, k1, k2)


def _oddeven_merge_sort_pairs(n):
    pairs = []
    p = 1
    while p < n:
        k = p
        while k >= 1:
            for j in range(k % p, n - k, 2 * k):
                for i in range(min(k, n - j - k)):
                    if (i + j) // (2 * p) == (i + j + k) // (2 * p):
                        pairs.append((i + j, i + j + k))
            k //= 2
        p *= 2
    return pairs


def _bitonic_merge_pairs(n):
    pairs = []
    k = n // 2
    while k >= 1:
        pairs += [(i, i + k) for i in range(n) if (i // k) % 2 == 0]
        k //= 2
    return pairs


_SORT_PAIRS = _oddeven_merge_sort_pairs(PEER_TOPK)
_MERGE_PAIRS = _bitonic_merge_pairs(PEER_TOPK)


def _exchange(v, pairs):
    v = list(v)
    for i, j in pairs:
        v[i], v[j] = jnp.maximum(v[i], v[j]), jnp.minimum(v[i], v[j])
    return v


def _key_tile(a):
    start = a * PEER_HEADS
    return pl.ds(start if isinstance(start, int) else pl.multiple_of(start, PEER_HEADS), PEER_HEADS)


def _top_values(s_ref, s_scr, g_scr, v_scr):
    nh, nk, k = PEER_HEADS, PEER_NKEYS, PEER_TOPK
    n_groups = nk // k
    for grp in range(n_groups):
        vals = []
        for i in range(k):
            a = grp * k + i
            t = s_ref[pl.ds(a, nh, stride=nk), :]
            s_scr[a * nh:(a + 1) * nh, :] = t
            vals.append(t)
        for i, t in enumerate(_exchange(vals, _SORT_PAIRS)):
            g_scr[(grp * k + i) * nh:(grp * k + i + 1) * nh, :] = t
    span = 1
    while span < n_groups:
        for grp in range(0, n_groups, 2 * span):
            lo = [g_scr[(grp * k + i) * nh:(grp * k + i + 1) * nh, :] for i in range(k)]
            hi = [g_scr[((grp + span) * k + i) * nh:((grp + span) * k + i + 1) * nh, :] for i in range(k)]
            top = _exchange([jnp.maximum(lo[i], hi[k - 1 - i]) for i in range(k)], _MERGE_PAIRS)
            for i, t in enumerate(top):
                g_scr[(grp * k + i) * nh:(grp * k + i + 1) * nh, :] = t
        span *= 2
    for i in range(k):
        v_scr[i] = g_scr[i * nh:(i + 1) * nh, :]


def _ties(s_scr, v_scr):
    k = PEER_TOPK
    v = [v_scr[i] for i in range(k)]
    bad = jnp.zeros_like(v[0])
    for i in range(k - 1):
        bad = bad + jnp.where(v[i] == v[i + 1], 1.0, 0.0)

    def count(a, c):
        return c + jnp.where(s_scr[_key_tile(a), :] >= v[k - 1], 1.0, 0.0)

    n_ge = lax.fori_loop(0, PEER_NKEYS, count, jnp.zeros_like(v[0]), unroll=8)
    return bad + jnp.where(n_ge == float(k), 0.0, 1.0)


def _rank_keys(s_scr, r_scr):
    nh, nk = PEER_HEADS, PEER_NKEYS
    bt = s_scr.shape[1]

    def init(a, m):
        r_scr[_key_tile(a), :] = jnp.full((nh, bt), NOT_SELECTED, F32)
        return jnp.maximum(m, s_scr[_key_tile(a), :])

    m0 = lax.fori_loop(0, nk, init, jnp.full((nh, bt), -jnp.inf, F32), unroll=8)

    def round_body(r, m):
        def find(a, idx):
            return jnp.minimum(idx, jnp.where(s_scr[_key_tile(a), :] == m, lax.convert_element_type(a, F32),
                                              float(nk)))

        idx = lax.fori_loop(0, nk, find, jnp.full((nh, bt), float(nk), F32), unroll=8)
        rank = lax.convert_element_type(r, F32)

        def knock(a, m_next):
            hit = idx == lax.convert_element_type(a, F32)
            s_new = jnp.where(hit, -jnp.inf, s_scr[_key_tile(a), :])
            s_scr[_key_tile(a), :] = s_new
            r_scr[_key_tile(a), :] = jnp.where(hit, rank, r_scr[_key_tile(a), :])
            return jnp.maximum(m_next, s_new)

        return lax.fori_loop(0, nk, knock, jnp.full((nh, bt), -jnp.inf, F32), unroll=8)

    lax.fori_loop(0, PEER_TOPK, round_body, m0)


def _topk_kernel(s1_ref, s2_ref, c1_ref, e1_ref, r2_ref, e2_ref,
                 s1_scr, s2_scr, g_scr, r2_scr, e2k_scr, v1_scr, v2_scr, c_scr, e1z_scr, e2_scr, t_scr):
    nh, nk, k = PEER_HEADS, PEER_NKEYS, PEER_TOPK
    _top_values(s1_ref, s1_scr, g_scr, v1_scr)
    _top_values(s2_ref, s2_scr, g_scr, v2_scr)

    v1 = [v1_scr[i] for i in range(PEER_TOPK)]
    v2 = [v2_scr[j] for j in range(PEER_TOPK)]
    cand = {(i, j): v1[i] + v2[j] for (i, j) in _CELLS}
    before = {p: 0.0 for p in _CELLS}
    for x, p in enumerate(_CELLS):
        for q in _CELLS[x + 1:]:
            if p[0] <= q[0] and p[1] <= q[1]:
                before[q] = before[q] + 1.0
            elif not (q[0] <= p[0] and q[1] <= p[1]):
                p_first = jnp.where(cand[p] >= cand[q], 1.0, 0.0)
                before[q] = before[q] + p_first
                before[p] = before[p] + (1.0 - p_first)
    e1 = [jnp.exp(v1[i] - v1[0]) for i in range(PEER_TOPK)]
    e2 = [jnp.exp(v2[j] - v2[0]) for j in range(PEER_TOPK)]
    height = [jnp.zeros_like(v1[0]) for _ in range(PEER_TOPK)]
    z = jnp.zeros_like(v1[0])
    for (i, j) in _CELLS:
        sel = jnp.where(before[(i, j)] < float(PEER_TOPK), 1.0, 0.0)
        height[i] = height[i] + sel
        z = z + sel * (e1[i] * e2[j])
    inv_z = 1.0 / z
    for i in range(PEER_TOPK):
        c_scr[i] = height[i]
        e1z_scr[i] = e1[i] * inv_z
        e2_scr[i] = e2[i]
    for c in range(1, k + 1):
        t = jnp.full_like(v1[0], jnp.inf)
        for i in range(k):
            t = jnp.minimum(t, jnp.where(height[i] >= float(c), v1[i], jnp.inf))
        t_scr[c - 1] = t

    n_ties = jnp.sum(_ties(s1_scr, v1_scr) + _ties(s2_scr, v2_scr))

    @pl.when(n_ties == 0.0)
    def _():
        v1_max = v1_scr[0]
        v2_max = v2_scr[0]
        inv = e1z_scr[0]

        def emit1(a, carry):
            s = s1_scr[_key_tile(a), :]
            c = jnp.zeros_like(s)
            for lvl in range(k):
                c = c + jnp.where(s >= t_scr[lvl], 1.0, 0.0)
            c1_ref[a] = c
            e1_ref[a] = jnp.exp(s - v1_max) * inv
            return carry

        lax.fori_loop(0, nk, emit1, 0, unroll=4)

        def emit2(b, carry):
            s = s2_scr[_key_tile(b), :]
            r = jnp.zeros_like(s)
            for i in range(k):
                r = r + jnp.where(v2_scr[i] > s, 1.0, 0.0)
            r2_scr[_key_tile(b), :] = r
            e2k_scr[_key_tile(b), :] = jnp.exp(s - v2_max)
            return carry

        lax.fori_loop(0, nk, emit2, 0, unroll=4)

    @pl.when(n_ties != 0.0)
    def _():
        _rank_keys(s1_scr, g_scr)

        def emit1(a, carry):
            rk = g_scr[_key_tile(a), :]
            c = jnp.zeros_like(rk)
            e = jnp.zeros_like(rk)
            for r in range(k):
                eq = rk == float(r)
                c = jnp.where(eq, c_scr[r], c)
                e = jnp.where(eq, e1z_scr[r], e)
            c1_ref[a] = c
            e1_ref[a] = e
            return carry

        lax.fori_loop(0, nk, emit1, 0, unroll=4)
        _rank_keys(s2_scr, r2_scr)

        def emit2(b, carry):
            rk = r2_scr[_key_tile(b), :]
            e = jnp.zeros_like(rk)
            for r in range(k):
                e = jnp.where(rk == float(r), e2_scr[r], e)
            e2k_scr[_key_tile(b), :] = e
            return carry

        lax.fori_loop(0, nk, emit2, 0, unroll=4)

    for hd in range(nh):
        r2_ref[hd] = r2_scr[pl.ds(hd, nk, stride=nh), :]
        e2_ref[hd] = e2k_scr[pl.ds(hd, nk, stride=nh), :]


def _topk(s1, s2):
    nh, nk, n = s1.shape
    bt = LANES
    rows = nh * nk
    big = pltpu.VMEM((rows, bt), F32)
    small = pltpu.VMEM((PEER_TOPK, nh, bt), F32)
    return pl.pallas_call(
        _topk_kernel,
        grid=(n // bt,),
        in_specs=[pl.BlockSpec((rows, bt), lambda i: (0, i)), pl.BlockSpec((rows, bt), lambda i: (0, i))],
        out_specs=[
            pl.BlockSpec((nk, nh, bt), lambda i: (0, 0, i)),
            pl.BlockSpec((nk, nh, bt), lambda i: (0, 0, i)),
            pl.BlockSpec((nh, nk, bt), lambda i: (0, 0, i)),
            pl.BlockSpec((nh, nk, bt), lambda i: (0, 0, i)),
        ],
        out_shape=[jax.ShapeDtypeStruct((nk, nh, n), F32), jax.ShapeDtypeStruct((nk, nh, n), F32),
                   jax.ShapeDtypeStruct((nh, nk, n), F32), jax.ShapeDtypeStruct((nh, nk, n), F32)],
        scratch_shapes=[big] * 5 + [small] * 6,
        compiler_params=_params(("parallel",), 40),
        name="peer_topk",
    )(s1.reshape(rows, n), s2.reshape(rows, n))


def _peer_ffn_kernel(xnt_ref, u_ref, v_ref, c1_ref, e1_ref, r2_ref, e2_ref, out_ref, w_scr):
    nk = PEER_NKEYS
    j = pl.program_id(1)
    last = pl.num_programs(1) - 1
    bt = out_ref.shape[1]

    def accumulate():
        out_ref[...] += lax.dot_general(v_ref[...], w_scr[(j + 1) % 2], (((0,), (0,)), ((), ())),
                                        preferred_element_type=F32)

    def weigh():
        at = jnp.dot(u_ref[...], xnt_ref[...], preferred_element_type=F32)
        cur = j % 2
        for al in range(A_PER_BLOCK):
            rows = slice(al * nk, (al + 1) * nk)
            for l0 in range(0, bt, LANES):
                lanes = slice(l0, l0 + LANES)
                g = jnp.zeros((nk, LANES), F32)
                for hd in range(PEER_HEADS):
                    sel = r2_ref[hd, :, lanes] < c1_ref[al, hd:hd + 1, lanes]
                    g = g + jnp.where(sel, e1_ref[al, hd:hd + 1, lanes] * e2_ref[hd, :, lanes], 0.0)
                w_scr[cur, rows, lanes] = (g * jax.nn.gelu(at[rows, lanes])).astype(BF16)

    @pl.when(j == 0)
    def _():
        out_ref[...] = jnp.zeros_like(out_ref)
        weigh()

    @pl.when(jnp.logical_and(j > 0, j < last))
    def _():
        accumulate()
        weigh()

    @pl.when(j == last)
    def _():
        accumulate()


def _peer_ffn(xnt, u, v, c1, e1, r2, e2):
    d, n = xnt.shape
    n_exp = u.shape[0]
    nk, nh, _ = c1.shape
    bt = min(512, n)
    be = A_PER_BLOCK * nk
    nj = n_exp // be
    once = pl.Buffered(1)
    return pl.pallas_call(
        _peer_ffn_kernel,
        grid=(n // bt, nj + 1),
        in_specs=[
            pl.BlockSpec((d, bt), lambda i, j: (0, i), pipeline_mode=once),
            pl.BlockSpec((be, d), lambda i, j: (jnp.minimum(j, nj - 1), 0)),
            pl.BlockSpec((be, d), lambda i, j: (jnp.maximum(j - 1, 0), 0)),
            pl.BlockSpec((A_PER_BLOCK, nh, bt), lambda i, j: (jnp.minimum(j, nj - 1), 0, i)),
            pl.BlockSpec((A_PER_BLOCK, nh, bt), lambda i, j: (jnp.minimum(j, nj - 1), 0, i)),
            pl.BlockSpec((nh, nk, bt), lambda i, j: (0, 0, i), pipeline_mode=once),
            pl.BlockSpec((nh, nk, bt), lambda i, j: (0, 0, i), pipeline_mode=once),
        ],
        out_specs=pl.BlockSpec((d, bt), lambda i, j: (0, i)),
        out_shape=jax.ShapeDtypeStruct((d, n), F32),
        scratch_shapes=[pltpu.VMEM((2, be, bt), BF16)],
        compiler_params=_params(("parallel", "arbitrary"), 56),
        name="peer_ffn",
    )(xnt, u, v, c1, e1, r2, e2)


def _add_t_kernel(h_ref, ot_ref, o_ref):
    o_ref[...] = h_ref[...] + ot_ref[...].T


def _add_transposed(h, out_t):
    n, d = h.shape
    bm = min(256, n)
    return pl.pallas_call(
        _add_t_kernel,
        grid=(n // bm,),
        in_specs=[pl.BlockSpec((bm, d), lambda i: (i, 0)), pl.BlockSpec((d, bm), lambda i: (0, i))],
        out_specs=pl.BlockSpec((bm, d), lambda i: (i, 0)),
        out_shape=jax.ShapeDtypeStruct((n, d), F32),
        compiler_params=_params(("parallel",), 56),
        name="add_transposed",
    )(h, out_t)


def _ple_kernel(h_ref, g_ref, wg_ref, p_ref, wp_ref, hcol_ref, o_ref, xn_ref):
    @pl.when(pl.program_id(1) == 0)
    def _():
        def store(rows, xn):
            xn_ref[rows, :] = xn.astype(BF16)

        _rms_row_groups(h_ref, g_ref, store)

    gate = jax.nn.sigmoid(jnp.dot(xn_ref[...], wg_ref[...], preferred_element_type=F32))
    pp = jnp.dot(p_ref[...], wp_ref[...], preferred_element_type=F32)
    o_ref[...] = hcol_ref[...] + pp * gate


def _ple(h, g, w_gate, p, w_proj):
    n, d = h.shape
    bm = min(1024, n)
    bn = 512
    pd = p.shape[1]
    return pl.pallas_call(
        _ple_kernel,
        grid=(n // bm, d // bn),
        in_specs=[
            pl.BlockSpec((bm, d), lambda i, j: (i, 0), pipeline_mode=pl.Buffered(1)),
            pl.BlockSpec((1, d), lambda i, j: (0, 0)),
            pl.BlockSpec((d, bn), lambda i, j: (0, j)),
            pl.BlockSpec((bm, pd), lambda i, j: (i, 0)),
            pl.BlockSpec((pd, bn), lambda i, j: (0, j)),
            pl.BlockSpec((bm, bn), lambda i, j: (i, j)),
        ],
        out_specs=pl.BlockSpec((bm, bn), lambda i, j: (i, j)),
        out_shape=jax.ShapeDtypeStruct((n, d), F32),
        scratch_shapes=[pltpu.VMEM((bm, d), BF16)],
        compiler_params=_params(("parallel", "arbitrary"), 56),
        name="ple",
    )(h, g, w_gate, p, w_proj, h)


def _final_norm_kernel(h_ref, g_ref, o_ref):
    def store(rows, xn):
        o_ref[rows, :] = xn

    _rms_row_groups(h_ref, g_ref, store)


def _final_norm(h, g):
    n, d = h.shape
    bm = min(512, n)
    return pl.pallas_call(
        _final_norm_kernel,
        grid=(n // bm,),
        in_specs=[pl.BlockSpec((bm, d), lambda i: (i, 0)), pl.BlockSpec((1, d), lambda i: (0, 0))],
        out_specs=pl.BlockSpec((bm, d), lambda i: (i, 0)),
        out_shape=jax.ShapeDtypeStruct((n, d), F32),
        compiler_params=_params(("parallel",), 48),
        name="final_norm",
    )(h, g)


def _cast_kernel(x_ref, o_ref):
    o_ref[...] = x_ref[...].astype(o_ref.dtype)


def _cast_layer(x, layer):
    _, r, c = x.shape
    bm = min(CAST_ROWS, r)
    return pl.pallas_call(
        _cast_kernel,
        grid=(r // bm,),
        in_specs=[pl.BlockSpec((None, bm, c), lambda i: (layer, i, 0))],
        out_specs=pl.BlockSpec((bm, c), lambda i: (i, 0)),
        out_shape=jax.ShapeDtypeStruct((r, c), BF16),
        compiler_params=_params(("parallel",), 40),
        name="cast_weight",
    )(x)


def _w_in_t_kernel(a_ref, b_ref, o_ref, lr_ref, *, n_lo):
    k = pl.program_id(0)
    bm = o_ref.shape[0]

    @pl.when(k == n_lo - 1)
    def _():
        lr_ref[...] = jnp.zeros_like(lr_ref)
        lr_ref[0:GLA_LR, :] = b_ref[...].astype(BF16)

    @pl.when(k < n_lo)
    def _():
        o_ref[...] = a_ref[...].astype(BF16)

    @pl.when(k >= n_lo)
    def _():
        o_ref[0:bm - GLA_LR, :] = a_ref[GLA_LR:bm, :].astype(BF16)
        o_ref[bm - GLA_LR:bm, :] = b_ref[...].astype(BF16)


def _split_w_in_t(w_in_t, layer):
    _, cols, d = w_in_t.shape
    lr0 = 2 * (d // 4) + 2 * (d // 2)
    bm = CAST_ROWS
    n_out = cols - GLA_LR
    assert lr0 % bm == 0 and n_out % bm == 0 and bm % GLA_LR == 0
    return pl.pallas_call(
        functools.partial(_w_in_t_kernel, n_lo=lr0 // bm),
        grid=(n_out // bm,),
        in_specs=[pl.BlockSpec((None, bm, d), lambda k: (layer, k, 0)),
                  pl.BlockSpec((None, GLA_LR, d), lambda k: (layer, (k + 1) * (bm // GLA_LR), 0))],
        out_specs=[pl.BlockSpec((bm, d), lambda k: (k, 0)), pl.BlockSpec((LANES, d), lambda k: (0, 0))],
        out_shape=[jax.ShapeDtypeStruct((n_out, d), BF16), jax.ShapeDtypeStruct((LANES, d), BF16)],
        compiler_params=_params(("arbitrary",), 40),
        name="split_w_in",
    )(w_in_t, w_in_t)


def _prep_layer(i, g_mix, w_in, gla_w_lr, gla_b_lr, gla_g_out, conf_w_dw, conf_b_dw, conf_ln_g, conf_ln_b,
                sc_w_dw, w_out, g_ffn, peer_w_q, peer_k1, peer_k2, peer_u, peer_v, g_ple, ple_w_gate, ple_w_proj):
    row = lambda x: x[i][None, :]
    w_main, w_lr_in = _split_w_in_t(jnp.swapaxes(w_in, 1, 2), i)
    return dict(
        g_mix=row(g_mix), w_main=w_main, w_lr_in=w_lr_in,
        w_lr=jnp.pad(gla_w_lr[i], ((0, LANES - GLA_LR), (0, 0))).astype(BF16),
        b_lr=row(gla_b_lr), gg=row(gla_g_out),
        wc=conf_w_dw[i], bc=row(conf_b_dw), lg=row(conf_ln_g), lb=row(conf_ln_b), ws=sc_w_dw[i],
        w_out=_cast_layer(w_out, i),
        g_ffn=row(g_ffn), wq_t=peer_w_q[i].T.astype(BF16),
        k1=peer_k1[i].astype(BF16), k2=peer_k2[i].astype(BF16),
        u=_cast_layer(peer_u, i), v=_cast_layer(peer_v, i),
        g_ple=row(g_ple), w_gate=_cast_layer(ple_w_gate, i), w_proj=ple_w_proj[i].astype(BF16),
    )


def _layer(h, p, s_gla, s_conf, s_sc, lw, batch, seq):
    proj, la = _in_proj(h, lw["g_mix"], lw["w_main"], lw["w_lr_in"], lw["w_lr"], lw["b_lr"])
    y_gla, new_gla = _gla(proj, la, lw["gg"], s_gla, batch, seq)
    y_conf, y_sc, new_conf, new_sc = _convs(proj, s_conf, s_sc, lw["wc"], lw["bc"], lw["lg"], lw["lb"], lw["ws"],
                                            batch, seq)
    h = _out_proj(y_gla, y_conf, y_sc, lw["w_out"], h)
    xnt, s1, s2 = _scores(h, lw["g_ffn"], lw["wq_t"], lw["k1"], lw["k2"])
    c1, e1, r2, e2 = _topk(s1, s2)
    h = _add_transposed(h, _peer_ffn(xnt, lw["u"], lw["v"], c1, e1, r2, e2))
    h = _ple(h, lw["g_ple"], lw["w_gate"], p.astype(BF16), lw["w_proj"])
    return h, new_gla, new_conf, new_sc


def _trunk(x, p, s_gla, s_conf, s_sc, layers, g_final):
    batch, seq, d = x.shape
    h = x.reshape(batch * seq, d)
    new_gla, new_conf, new_sc = [], [], []
    for i, lw in enumerate(layers):
        st = (None, None, None) if s_gla is None else (s_gla[i], s_conf[i], s_sc[i])
        h, sg, scf, ssc = _layer(h, p[i].reshape(batch * seq, -1), *st, lw, batch, seq)
        new_gla.append(sg)
        new_conf.append(scf)
        new_sc.append(ssc)
    y = _final_norm(h, g_final[None, :]).reshape(batch, seq, d)
    return y, jnp.stack(new_gla), jnp.stack(new_conf), jnp.stack(new_sc)


def kernel(x_prompt, x_sample, state_gla, state_conf, state_sconv, p_prompt, p_sample, g_mix, w_in, gla_w_lr, gla_b_lr, gla_g_out, conf_w_dw, conf_b_dw, conf_ln_g, conf_ln_b, sc_w_dw, w_out, g_ffn, peer_w_q, peer_k1, peer_k2, peer_u, peer_v, g_ple, ple_w_gate, ple_w_proj, g_final):
    layers = [_prep_layer(i, g_mix, w_in, gla_w_lr, gla_b_lr, gla_g_out, conf_w_dw, conf_b_dw, conf_ln_g,
                          conf_ln_b, sc_w_dw, w_out, g_ffn, peer_w_q, peer_k1, peer_k2, peer_u, peer_v, g_ple,
                          ple_w_gate, ple_w_proj) for i in range(w_in.shape[0])]
    y_p, gla_p, conf_p, sc_p = _trunk(x_prompt, p_prompt, None, None, None, layers, g_final)
    y_s, gla_s, conf_s, sc_s = _trunk(x_sample, p_sample, state_gla, state_conf, state_sconv, layers, g_final)
    return (y_p, y_s, gla_p, conf_p, sc_p, gla_s, conf_s, sc_s)
```

```python
import functools

import jax
import jax.numpy as jnp
from jax import lax
from jax.experimental import pallas as pl
from jax.experimental.pallas import tpu as pltpu

F32 = jnp.float32
BF16 = jnp.bfloat16

EPS = 1e-6
CHUNK = 64
GLA_HEADS = 4
GLA_LR = 16
GLA_TAU = 16.0
CONF_K = 31
SC_K = 3
PEER_HEADS = 8
PEER_NKEYS = 128
PEER_TOPK = 16

LANES = 128
SUBLANES = 8
CONF_HIST = 32
SC_HIST = 8
NORM_ROWS = 128
CAST_ROWS = 512
GLA_ROWS = 256
CONV_ROWS = 256
A_PER_BLOCK = 4
NOT_SELECTED = 64.0
MIB = 1024 * 1024

_CELLS = [(i, j) for i in range(PEER_TOPK) for j in range(PEER_TOPK) if (i + 1) * (j + 1) <= PEER_TOPK]


def _params(semantics, vmem_mib, flags=None):
    return pltpu.CompilerParams(dimension_semantics=semantics, vmem_limit_bytes=vmem_mib * MIB, flags=flags)


def _rms(x, g):
    ms = jnp.mean(x * x, axis=-1, keepdims=True)
    return x * lax.rsqrt(ms + EPS) * g


def _rms_row_groups(h_ref, g_ref, store):
    g = g_ref[...]
    for r0 in range(0, h_ref.shape[0], NORM_ROWS):
        rows = slice(r0, min(r0 + NORM_ROWS, h_ref.shape[0]))
        store(rows, _rms(h_ref[rows, :], g))


def _in_proj_kernel(h_ref, g_ref, w_ref, wlr_in_ref, wlr_ref, blr_ref, proj_ref, la_ref, xn_ref):
    nt = (((1,), (1,)), ((), ()))

    @pl.when(pl.program_id(1) == 0)
    def _():
        def store(rows, xn):
            xn = xn.astype(BF16)
            xn_ref[rows, :] = xn
            lr = lax.dot_general(xn, wlr_in_ref[...], nt, preferred_element_type=F32)
            z = jnp.dot(lr.astype(BF16), wlr_ref[...], preferred_element_type=F32) + blr_ref[...]
            la_ref[rows, :] = (jnp.minimum(z, 0.0) - jnp.log1p(jnp.exp(-jnp.abs(z)))) * (1.0 / GLA_TAU)

        _rms_row_groups(h_ref, g_ref, store)

    proj_ref[...] = lax.dot_general(xn_ref[...], w_ref[...], nt, preferred_element_type=F32)


def _in_proj(h, g, w_main, w_lr_in, w_lr, b_lr):
    n, d = h.shape
    ncols = w_main.shape[0]
    qk = w_lr.shape[1]
    bm = min(1024, n)
    bn = d // 8
    return pl.pallas_call(
        _in_proj_kernel,
        grid=(n // bm, ncols // bn),
        in_specs=[
            pl.BlockSpec((bm, d), lambda i, j: (i, 0), pipeline_mode=pl.Buffered(1)),
            pl.BlockSpec((1, d), lambda i, j: (0, 0)),
            pl.BlockSpec((bn, d), lambda i, j: (j, 0)),
            pl.BlockSpec((LANES, d), lambda i, j: (0, 0)),
            pl.BlockSpec((LANES, qk), lambda i, j: (0, 0)),
            pl.BlockSpec((1, qk), lambda i, j: (0, 0)),
        ],
        out_specs=[
            pl.BlockSpec((bm, bn), lambda i, j: (i, j)),
            pl.BlockSpec((bm, qk), lambda i, j: (i, 0)),
        ],
        out_shape=[jax.ShapeDtypeStruct((n, ncols), F32), jax.ShapeDtypeStruct((n, qk), F32)],
        scratch_shapes=[pltpu.VMEM((bm, d), BF16)],
        compiler_params=_params(("parallel", "arbitrary"), 56),
        name="in_proj",
    )(h, g, w_main, w_lr_in, w_lr, b_lr)


def _gla_kernel(*refs, chunk, n_chunks, has_state, scale):
    if has_state:
        q_ref, k_ref, v_ref, g_ref, la_ref, gg_ref, s0_ref, y_ref, sout_ref, st_ref = refs
    else:
        q_ref, k_ref, v_ref, g_ref, la_ref, gg_ref, y_ref, sout_ref, st_ref = refs
    tt = pl.program_id(1)
    n_heads, dv, dk = st_ref.shape

    @pl.when(tt == 0)
    def _():
        for hd in range(n_heads):
            st_ref[hd] = s0_ref[0, hd].T if has_state else jnp.zeros((dv, dk), F32)

    row = lax.broadcasted_iota(jnp.int32, (chunk, chunk), 0)
    col = lax.broadcasted_iota(jnp.int32, (chunk, chunk), 1)
    causal = row >= col
    tril = jnp.where(causal, 1.0, 0.0).astype(BF16)
    mid = (chunk - 1) // 2
    nt = (((1,), (1,)), ((), ()))
    tn = (((0,), (0,)), ((), ()))

    for c in range(n_chunks):
        sl = pl.ds(c * chunk, chunk)
        la = la_ref[sl, :]
        la_hi = la.astype(BF16)
        r1 = la - la_hi.astype(F32)
        la_mid = r1.astype(BF16)
        la_lo = (r1 - la_mid.astype(F32)).astype(BF16)
        cb_all = (jnp.dot(tril, la_hi, preferred_element_type=F32)
                  + jnp.dot(tril, la_mid, preferred_element_type=F32)
                  + jnp.dot(tril, la_lo, preferred_element_type=F32))
        for hd in range(n_heads):
            kc = slice(hd * dk, (hd + 1) * dk)
            vc = slice(hd * dv, (hd + 1) * dv)
            cb = cb_all[:, kc]
            c_last = cb[chunk - 1:chunk, :]
            c_mid = cb[mid:mid + 1, :]
            q = q_ref[sl, kc] * scale
            k = k_ref[sl, kc]
            vb = v_ref[sl, vc].astype(BF16)
            st = st_ref[hd]
            qe = (q * jnp.exp(cb - c_mid)).astype(BF16)
            ke = (k * jnp.exp(c_mid - cb)).astype(BF16)
            att = lax.dot_general(qe, ke, nt, preferred_element_type=F32)
            att = jnp.where(causal, att, 0.0).astype(BF16)
            q2 = (q * jnp.exp(cb)).astype(BF16)
            o = (jnp.dot(att, vb, preferred_element_type=F32)
                 + lax.dot_general(q2, st.astype(BF16), nt, preferred_element_type=F32))
            k2 = (k * jnp.exp(c_last - cb)).astype(BF16)
            st_ref[hd] = jnp.exp(c_last) * st + lax.dot_general(vb, k2, tn, preferred_element_type=F32)
            o = o * lax.rsqrt(jnp.mean(o * o, axis=-1, keepdims=True) + EPS)
            gate = g_ref[sl, vc]
            y_ref[sl, vc] = (o * gg_ref[:, vc] * (gate * jax.nn.sigmoid(gate))).astype(y_ref.dtype)

    @pl.when(tt == pl.num_programs(1) - 1)
    def _():
        for hd in range(n_heads):
            sout_ref[0, hd] = st_ref[hd].T


def _gla(proj, la, gg, s0, batch, seq):
    n = proj.shape[0]
    qk = la.shape[1]
    v_w = gg.shape[1]
    dk = qk // GLA_HEADS
    dv = v_w // GLA_HEADS
    chunk = min(CHUNK, seq)
    tb = min(seq, GLA_ROWS)
    assert seq % tb == 0 and tb % chunk == 0 and 2 * qk % v_w == 0
    nt = seq // tb
    has_state = s0 is not None
    rows = lambda b, t: b * nt + t
    v_blk = 2 * qk // v_w
    in_specs = [
        pl.BlockSpec((tb, qk), lambda b, t: (rows(b, t), 0)),
        pl.BlockSpec((tb, qk), lambda b, t: (rows(b, t), 1)),
        pl.BlockSpec((tb, v_w), lambda b, t: (rows(b, t), v_blk)),
        pl.BlockSpec((tb, v_w), lambda b, t: (rows(b, t), v_blk + 1)),
        pl.BlockSpec((tb, qk), lambda b, t: (rows(b, t), 0)),
        pl.BlockSpec((1, v_w), lambda b, t: (0, 0)),
    ]
    args = [proj, proj, proj, proj, la, gg]
    if has_state:
        in_specs.append(pl.BlockSpec((1, GLA_HEADS, dk, dv), lambda b, t: (b, 0, 0, 0)))
        args.append(s0)
    return pl.pallas_call(
        functools.partial(_gla_kernel, chunk=chunk, n_chunks=tb // chunk, has_state=has_state, scale=dk ** -0.5),
        grid=(batch, nt),
        in_specs=in_specs,
        out_specs=[
            pl.BlockSpec((tb, v_w), lambda b, t: (rows(b, t), 0)),
            pl.BlockSpec((1, GLA_HEADS, dk, dv), lambda b, t: (b, 0, 0, 0)),
        ],
        out_shape=[jax.ShapeDtypeStruct((n, v_w), BF16),
                   jax.ShapeDtypeStruct((batch, GLA_HEADS, dk, dv), F32)],
        scratch_shapes=[pltpu.VMEM((GLA_HEADS, dv, dk), F32)],
        compiler_params=_params(("parallel", "arbitrary"), 48),
        name="gla",
    )(*args)


def _conv_kernel(*refs, tb, has_state):
    if has_state:
        (ca_ref, cg_ref, sb_ref, scg_ref, sh_ref, wc_ref, bc_ref, lg_ref, lb_ref, ws_ref, hc_ref, hs_ref,
         yc_ref, ys_ref, nc_ref, ns_ref, ubuf, zbuf, sbuf) = refs
    else:
        (ca_ref, cg_ref, sb_ref, scg_ref, sh_ref, wc_ref, bc_ref, lg_ref, lb_ref, ws_ref,
         yc_ref, ys_ref, nc_ref, ns_ref, ubuf, zbuf, sbuf) = refs
    tt = pl.program_id(1)
    c_skip = CONF_HIST - (CONF_K - 1)
    s_skip = SC_HIST - (SC_K - 1)

    @pl.when(tt == 0)
    def _():
        ubuf[0:CONF_HIST, :] = jnp.zeros((CONF_HIST, ubuf.shape[1]), F32)
        zbuf[0:SC_HIST, :] = jnp.zeros((SC_HIST, zbuf.shape[1]), F32)
        if has_state:
            ubuf[c_skip:CONF_HIST, :] = hc_ref[0]
            zbuf[s_skip:SC_HIST, :] = hs_ref[0]

    @pl.when(tt > 0)
    def _():
        ubuf[0:CONF_HIST, :] = ubuf[tb:tb + CONF_HIST, :]
        zbuf[0:SC_HIST, :] = zbuf[tb:tb + SC_HIST, :]

    ubuf[CONF_HIST:CONF_HIST + tb, :] = ca_ref[...] * jax.nn.sigmoid(cg_ref[...])
    zbuf[SC_HIST:SC_HIST + tb, :] = scg_ref[...] * sh_ref[...]

    span = tb + CONF_HIST - SUBLANES
    acc = None
    for r in range(SUBLANES):
        taps = [j for j in range(CONF_K) if (c_skip + j) % SUBLANES == r]
        if not taps:
            continue
        if r:
            sbuf[0:span, :] = ubuf[r:r + span, :]
        src = sbuf if r else ubuf
        for j in taps:
            base = c_skip + j - r
            term = wc_ref[j:j + 1, :] * src[base:base + tb, :]
            acc = term if acc is None else acc + term
    c = acc + bc_ref[...]
    mu = jnp.mean(c, axis=-1, keepdims=True)
    cc = c - mu
    var = jnp.mean(cc * cc, axis=-1, keepdims=True)
    y = cc * lax.rsqrt(var + EPS) * lg_ref[...] + lb_ref[...]
    yc_ref[...] = (y * jax.nn.sigmoid(y)).astype(yc_ref.dtype)

    z = ws_ref[0:1, :] * zbuf[s_skip:s_skip + tb, :]
    for j in range(1, SC_K):
        z = z + ws_ref[j:j + 1, :] * zbuf[s_skip + j:s_skip + j + tb, :]
    ys_ref[...] = (sb_ref[...] * z).astype(ys_ref.dtype)

    @pl.when(tt == pl.num_programs(1) - 1)
    def _():
        nc_ref[0] = ubuf[tb + c_skip:tb + CONF_HIST, :]
        ns_ref[0] = zbuf[tb + s_skip:tb + SC_HIST, :]


def _convs(proj, hc, hs, wc, bc, lg, lb, ws, batch, seq):
    n = proj.shape[0]
    w = wc.shape[1]
    tb = min(seq, CONV_ROWS)
    nt = seq // tb
    assert seq % tb == 0 and (nt == 1 or tb >= CONF_HIST)
    has_state = hc is not None
    first = (proj.shape[1] - 5 * w) // w
    rows = lambda b, t: b * nt + t
    col = lambda c: pl.BlockSpec((tb, w), lambda b, t: (rows(b, t), first + c))
    vec = lambda r: pl.BlockSpec((r, w), lambda b, t: (0, 0))
    in_specs = [col(0), col(1), col(2), col(3), col(4), vec(CONF_K), vec(1), vec(1), vec(1), vec(SC_K)]
    args = [proj] * 5 + [wc, bc, lg, lb, ws]
    if has_state:
        in_specs += [pl.BlockSpec((1, CONF_K - 1, w), lambda b, t: (b, 0, 0)),
                     pl.BlockSpec((1, SC_K - 1, w), lambda b, t: (b, 0, 0))]
        args += [hc, hs]
    return pl.pallas_call(
        functools.partial(_conv_kernel, tb=tb, has_state=has_state),
        grid=(batch, nt),
        in_specs=in_specs,
        out_specs=[
            pl.BlockSpec((tb, w), lambda b, t: (rows(b, t), 0)),
            pl.BlockSpec((tb, w), lambda b, t: (rows(b, t), 0)),
            pl.BlockSpec((1, CONF_K - 1, w), lambda b, t: (b, 0, 0)),
            pl.BlockSpec((1, SC_K - 1, w), lambda b, t: (b, 0, 0)),
        ],
        out_shape=[jax.ShapeDtypeStruct((n, w), BF16), jax.ShapeDtypeStruct((n, w), BF16),
                   jax.ShapeDtypeStruct((batch, CONF_K - 1, w), F32),
                   jax.ShapeDtypeStruct((batch, SC_K - 1, w), F32)],
        scratch_shapes=[pltpu.VMEM((CONF_HIST + tb, w), F32), pltpu.VMEM((SC_HIST + tb, w), F32),
                        pltpu.VMEM((CONF_HIST + tb, w), F32)],
        compiler_params=_params(("parallel", "arbitrary"), 32),
        name="convs",
    )(*args)


def _out_proj_kernel(yg_ref, yc_ref, ys_ref, w_ref, h_ref, o_ref):
    v_w = yg_ref.shape[1]
    c_w = yc_ref.shape[1]
    acc = jnp.dot(yg_ref[...], w_ref[0:v_w, :], preferred_element_type=F32)
    acc = acc + jnp.dot(yc_ref[...], w_ref[v_w:v_w + c_w, :], preferred_element_type=F32)
    acc = acc + jnp.dot(ys_ref[...], w_ref[v_w + c_w:, :], preferred_element_type=F32)
    o_ref[...] = h_ref[...] + acc


def _out_proj(yg, yc, ys, w_out, h):
    n, d = h.shape
    bm = min(1024, n)
    bn = 512
    return pl.pallas_call(
        _out_proj_kernel,
        grid=(n // bm, d // bn),
        in_specs=[
            pl.BlockSpec((bm, yg.shape[1]), lambda i, j: (i, 0)),
            pl.BlockSpec((bm, yc.shape[1]), lambda i, j: (i, 0)),
            pl.BlockSpec((bm, ys.shape[1]), lambda i, j: (i, 0)),
            pl.BlockSpec((w_out.shape[0], bn), lambda i, j: (0, j)),
            pl.BlockSpec((bm, bn), lambda i, j: (i, j)),
        ],
        out_specs=pl.BlockSpec((bm, bn), lambda i, j: (i, j)),
        out_shape=jax.ShapeDtypeStruct((n, d), F32),
        compiler_params=_params(("parallel", "arbitrary"), 48),
        name="out_proj",
    )(yg, yc, ys, w_out, h)


def _scores_kernel(h_ref, g_ref, wq_ref, k1_ref, k2_ref, xnt_ref, s1_ref, s2_ref):
    @pl.when(pl.program_id(1) == 0)
    def _():
        def store(rows, xn):
            xnt_ref[:, rows] = xn.T.astype(BF16)

        _rms_row_groups(h_ref, g_ref, store)

    q = jnp.dot(wq_ref[...], xnt_ref[...], preferred_element_type=F32)
    half = q.shape[0] // 2
    s1_ref[0] = jnp.dot(k1_ref[0], q[:half].astype(BF16), preferred_element_type=F32)
    s2_ref[0] = jnp.dot(k2_ref[0], q[half:].astype(BF16), preferred_element_type=F32)


def _scores(h, g, wq_t, k1, k2):
    n, d = h.shape
    bt = min(512, n)
    dq = wq_t.shape[0] // PEER_HEADS
    nk = k1.shape[1]
    return pl.pallas_call(
        _scores_kernel,
        grid=(n // bt, PEER_HEADS),
        in_specs=[
            pl.BlockSpec((bt, d), lambda i, hd: (i, 0)),
            pl.BlockSpec((1, d), lambda i, hd: (0, 0)),
            pl.BlockSpec((dq, d), lambda i, hd: (hd, 0)),
            pl.BlockSpec((1, nk, dq // 2), lambda i, hd: (hd, 0, 0)),
            pl.BlockSpec((1, nk, dq // 2), lambda i, hd: (hd, 0, 0)),
        ],
        out_specs=[
            pl.BlockSpec((d, bt), lambda i, hd: (0, i)),
            pl.BlockSpec((1, nk, bt), lambda i, hd: (hd, 0, i)),
            pl.BlockSpec((1, nk, bt), lambda i, hd: (hd, 0, i)),
        ],
        out_shape=[jax.ShapeDtypeStruct((d, n), BF16),
                   jax.ShapeDtypeStruct((PEER_HEADS, nk, n), F32),
                   jax.ShapeDtypeStruct((PEER_HEADS, nk, n), F32)],
        compiler_params=_params(("parallel", "arbitrary"), 48),
        name="peer_scores",
    )(h, g, wq_t, k1, k2)


def _oddeven_merge_sort_pairs(n):
    pairs = []
    p = 1
    while p < n:
        k = p
        while k >= 1:
            for j in range(k % p, n - k, 2 * k):
                for i in range(min(k, n - j - k)):
                    if (i + j) // (2 * p) == (i + j + k) // (2 * p):
                        pairs.append((i + j, i + j + k))
            k //= 2
        p *= 2
    return pairs


def _bitonic_merge_pairs(n):
    pairs = []
    k = n // 2
    while k >= 1:
        pairs += [(i, i + k) for i in range(n) if (i // k) % 2 == 0]
        k //= 2
    return pairs


_SORT_PAIRS = _oddeven_merge_sort_pairs(PEER_TOPK)
_MERGE_PAIRS = _bitonic_merge_pairs(PEER_TOPK)


def _exchange(v, pairs):
    v = list(v)
    for i, j in pairs:
        v[i], v[j] = jnp.maximum(v[i], v[j]), jnp.minimum(v[i], v[j])
    return v


def _key_tile(a):
    start = a * PEER_HEADS
    return pl.ds(start if isinstance(start, int) else pl.multiple_of(start, PEER_HEADS), PEER_HEADS)


def _top_values(s_ref, s_scr, g_scr, v_scr):
    nh, nk, k = PEER_HEADS, PEER_NKEYS, PEER_TOPK
    n_groups = nk // k
    for grp in range(n_groups):
        vals = []
        for i in range(k):
            a = grp * k + i
            t = s_ref[pl.ds(a, nh, stride=nk), :]
            s_scr[a * nh:(a + 1) * nh, :] = t
            vals.append(t)
        for i, t in enumerate(_exchange(vals, _SORT_PAIRS)):
            g_scr[(grp * k + i) * nh:(grp * k + i + 1) * nh, :] = t
    span = 1
    while span < n_groups:
        for grp in range(0, n_groups, 2 * span):
            lo = [g_scr[(grp * k + i) * nh:(grp * k + i + 1) * nh, :] for i in range(k)]
            hi = [g_scr[((grp + span) * k + i) * nh:((grp + span) * k + i + 1) * nh, :] for i in range(k)]
            top = _exchange([jnp.maximum(lo[i], hi[k - 1 - i]) for i in range(k)], _MERGE_PAIRS)
            for i, t in enumerate(top):
                g_scr[(grp * k + i) * nh:(grp * k + i + 1) * nh, :] = t
        span *= 2
    for i in range(k):
        v_scr[i] = g_scr[i * nh:(i + 1) * nh, :]


def _ties(s_scr, v_scr):
    k = PEER_TOPK
    v = [v_scr[i] for i in range(k)]
    bad = jnp.zeros_like(v[0])
    for i in range(k - 1):
        bad = bad + jnp.where(v[i] == v[i + 1], 1.0, 0.0)

    def count(a, c):
        return c + jnp.where(s_scr[_key_tile(a), :] >= v[k - 1], 1.0, 0.0)

    n_ge = lax.fori_loop(0, PEER_NKEYS, count, jnp.zeros_like(v[0]), unroll=8)
    return bad + jnp.where(n_ge == float(k), 0.0, 1.0)


def _rank_keys(s_scr, r_scr):
    nh, nk = PEER_HEADS, PEER_NKEYS
    bt = s_scr.shape[1]

    def init(a, m):
        r_scr[_key_tile(a), :] = jnp.full((nh, bt), NOT_SELECTED, F32)
        return jnp.maximum(m, s_scr[_key_tile(a), :])

    m0 = lax.fori_loop(0, nk, init, jnp.full((nh, bt), -jnp.inf, F32), unroll=8)

    def round_body(r, m):
        def find(a, idx):
            return jnp.minimum(idx, jnp.where(s_scr[_key_tile(a), :] == m, lax.convert_element_type(a, F32),
                                              float(nk)))

        idx = lax.fori_loop(0, nk, find, jnp.full((nh, bt), float(nk), F32), unroll=8)
        rank = lax.convert_element_type(r, F32)

        def knock(a, m_next):
            hit = idx == lax.convert_element_type(a, F32)
            s_new = jnp.where(hit, -jnp.inf, s_scr[_key_tile(a), :])
            s_scr[_key_tile(a), :] = s_new
            r_scr[_key_tile(a), :] = jnp.where(hit, rank, r_scr[_key_tile(a), :])
            return jnp.maximum(m_next, s_new)

        return lax.fori_loop(0, nk, knock, jnp.full((nh, bt), -jnp.inf, F32), unroll=8)

    lax.fori_loop(0, PEER_TOPK, round_body, m0)


def _topk_kernel(s1_ref, s2_ref, c1_ref, e1_ref, r2_ref, e2_ref,
                 s1_scr, s2_scr, g_scr, r2_scr, e2k_scr, v1_scr, v2_scr, c_scr, e1z_scr, e2_scr, t_scr):
    nh, nk, k = PEER_HEADS, PEER_NKEYS, PEER_TOPK
    _top_values(s1_ref, s1_scr, g_scr, v1_scr)
    _top_values(s2_ref, s2_scr, g_scr, v2_scr)

    v1 = [v1_scr[i] for i in range(PEER_TOPK)]
    v2 = [v2_scr[j] for j in range(PEER_TOPK)]
    cand = {(i, j): v1[i] + v2[j] for (i, j) in _CELLS}
    before = {p: 0.0 for p in _CELLS}
    for x, p in enumerate(_CELLS):
        for q in _CELLS[x + 1:]:
            if p[0] <= q[0] and p[1] <= q[1]:
                before[q] = before[q] + 1.0
            elif not (q[0] <= p[0] and q[1] <= p[1]):
                p_first = jnp.where(cand[p] >= cand[q], 1.0, 0.0)
                before[q] = before[q] + p_first
                before[p] = before[p] + (1.0 - p_first)
    e1 = [jnp.exp(v1[i] - v1[0]) for i in range(PEER_TOPK)]
    e2 = [jnp.exp(v2[j] - v2[0]) for j in range(PEER_TOPK)]
    height = [jnp.zeros_like(v1[0]) for _ in range(PEER_TOPK)]
    z = jnp.zeros_like(v1[0])
    for (i, j) in _CELLS:
        sel = jnp.where(before[(i, j)] < float(PEER_TOPK), 1.0, 0.0)
        height[i] = height[i] + sel
        z = z + sel * (e1[i] * e2[j])
    inv_z = 1.0 / z
    for i in range(PEER_TOPK):
        c_scr[i] = height[i]
        e1z_scr[i] = e1[i] * inv_z
        e2_scr[i] = e2[i]
    for c in range(1, k + 1):
        t = jnp.full_like(v1[0], jnp.inf)
        for i in range(k):
            t = jnp.minimum(t, jnp.where(height[i] >= float(c), v1[i], jnp.inf))
        t_scr[c - 1] = t

    n_ties = jnp.sum(_ties(s1_scr, v1_scr) + _ties(s2_scr, v2_scr))

    @pl.when(n_ties == 0.0)
    def _():
        v1_max = v1_scr[0]
        v2_max = v2_scr[0]
        inv = e1z_scr[0]

        def emit1(a, carry):
            s = s1_scr[_key_tile(a), :]
            c = jnp.zeros_like(s)
            for lvl in range(k):
                c = c + jnp.where(s >= t_scr[lvl], 1.0, 0.0)
            c1_ref[a] = c
            e1_ref[a] = jnp.exp(s - v1_max) * inv
            return carry

        lax.fori_loop(0, nk, emit1, 0, unroll=4)

        def emit2(b, carry):
            s = s2_scr[_key_tile(b), :]
            r = jnp.zeros_like(s)
            for i in range(k):
                r = r + jnp.where(v2_scr[i] > s, 1.0, 0.0)
            r2_scr[_key_tile(b), :] = r
            e2k_scr[_key_tile(b), :] = jnp.exp(s - v2_max)
            return carry

        lax.fori_loop(0, nk, emit2, 0, unroll=4)

    @pl.when(n_ties != 0.0)
    def _():
        _rank_keys(s1_scr, g_scr)

        def emit1(a, carry):
            rk = g_scr[_key_tile(a), :]
            c = jnp.zeros_like(rk)
            e = jnp.zeros_like(rk)
            for r in range(k):
                eq = rk == float(r)
                c = jnp.where(eq, c_scr[r], c)
                e = jnp.where(eq, e1z_scr[r], e)
            c1_ref[a] = c
            e1_ref[a] = e
            return carry

        lax.fori_loop(0, nk, emit1, 0, unroll=4)
        _rank_keys(s2_scr, r2_scr)

        def emit2(b, carry):
            rk = r2_scr[_key_tile(b), :]
            e = jnp.zeros_like(rk)
            for r in range(k):
                e = jnp.where(rk == float(r), e2_scr[r], e)
            e2k_scr[_key_tile(b), :] = e
            return carry

        lax.fori_loop(0, nk, emit2, 0, unroll=4)

    for hd in range(nh):
        r2_ref[hd] = r2_scr[pl.ds(hd, nk, stride=nh), :]
        e2_ref[hd] = e2k_scr[pl.ds(hd, nk, stride=nh), :]


def _topk(s1, s2):
    nh, nk, n = s1.shape
    bt = LANES
    rows = nh * nk
    big = pltpu.VMEM((rows, bt), F32)
    small = pltpu.VMEM((PEER_TOPK, nh, bt), F32)
    return pl.pallas_call(
        _topk_kernel,
        grid=(n // bt,),
        in_specs=[pl.BlockSpec((rows, bt), lambda i: (0, i)), pl.BlockSpec((rows, bt), lambda i: (0, i))],
        out_specs=[
            pl.BlockSpec((nk, nh, bt), lambda i: (0, 0, i)),
            pl.BlockSpec((nk, nh, bt), lambda i: (0, 0, i)),
            pl.BlockSpec((nh, nk, bt), lambda i: (0, 0, i)),
            pl.BlockSpec((nh, nk, bt), lambda i: (0, 0, i)),
        ],
        out_shape=[jax.ShapeDtypeStruct((nk, nh, n), F32), jax.ShapeDtypeStruct((nk, nh, n), F32),
                   jax.ShapeDtypeStruct((nh, nk, n), F32), jax.ShapeDtypeStruct((nh, nk, n), F32)],
        scratch_shapes=[big] * 5 + [small] * 6,
        compiler_params=_params(("parallel",), 40),
        name="peer_topk",
    )(s1.reshape(rows, n), s2.reshape(rows, n))


def _peer_ffn_kernel(xnt_ref, u_ref, v_ref, c1_ref, e1_ref, r2_ref, e2_ref, out_ref, w_scr):
    nk = PEER_NKEYS
    j = pl.program_id(1)
    last = pl.num_programs(1) - 1
    bt = out_ref.shape[1]

    def accumulate():
        out_ref[...] += lax.dot_general(v_ref[...], w_scr[(j + 1) % 2], (((0,), (0,)), ((), ())),
                                        preferred_element_type=F32)

    def weigh():
        at = jnp.dot(u_ref[...], xnt_ref[...], preferred_element_type=F32)
        cur = j % 2
        for al in range(A_PER_BLOCK):
            rows = slice(al * nk, (al + 1) * nk)
            for l0 in range(0, bt, LANES):
                lanes = slice(l0, l0 + LANES)
                g = jnp.zeros((nk, LANES), F32)
                for hd in range(PEER_HEADS):
                    sel = r2_ref[hd, :, lanes] < c1_ref[al, hd:hd + 1, lanes]
                    g = g + jnp.where(sel, e1_ref[al, hd:hd + 1, lanes] * e2_ref[hd, :, lanes], 0.0)
                w_scr[cur, rows, lanes] = (g * jax.nn.gelu(at[rows, lanes])).astype(BF16)

    @pl.when(j == 0)
    def _():
        out_ref[...] = jnp.zeros_like(out_ref)
        weigh()

    @pl.when(jnp.logical_and(j > 0, j < last))
    def _():
        accumulate()
        weigh()

    @pl.when(j == last)
    def _():
        accumulate()


def _peer_ffn(xnt, u, v, c1, e1, r2, e2):
    d, n = xnt.shape
    n_exp = u.shape[0]
    nk, nh, _ = c1.shape
    bt = min(512, n)
    be = A_PER_BLOCK * nk
    nj = n_exp // be
    once = pl.Buffered(1)
    return pl.pallas_call(
        _peer_ffn_kernel,
        grid=(n // bt, nj + 1),
        in_specs=[
            pl.BlockSpec((d, bt), lambda i, j: (0, i), pipeline_mode=once),
            pl.BlockSpec((be, d), lambda i, j: (jnp.minimum(j, nj - 1), 0)),
            pl.BlockSpec((be, d), lambda i, j: (jnp.maximum(j - 1, 0), 0)),
            pl.BlockSpec((A_PER_BLOCK, nh, bt), lambda i, j: (jnp.minimum(j, nj - 1), 0, i)),
            pl.BlockSpec((A_PER_BLOCK, nh, bt), lambda i, j: (jnp.minimum(j, nj - 1), 0, i)),
            pl.BlockSpec((nh, nk, bt), lambda i, j: (0, 0, i), pipeline_mode=once),
            pl.BlockSpec((nh, nk, bt), lambda i, j: (0, 0, i), pipeline_mode=once),
        ],
        out_specs=pl.BlockSpec((d, bt), lambda i, j: (0, i)),
        out_shape=jax.ShapeDtypeStruct((d, n), F32),
        scratch_shapes=[pltpu.VMEM((2, be, bt), BF16)],
        compiler_params=_params(("parallel", "arbitrary"), 56),
        name="peer_ffn",
    )(xnt, u, v, c1, e1, r2, e2)


def _add_t_kernel(h_ref, ot_ref, o_ref):
    o_ref[...] = h_ref[...] + ot_ref[...].T


def _add_transposed(h, out_t):
    n, d = h.shape
    bm = min(256, n)
    return pl.pallas_call(
        _add_t_kernel,
        grid=(n // bm,),
        in_specs=[pl.BlockSpec((bm, d), lambda i: (i, 0)), pl.BlockSpec((d, bm), lambda i: (0, i))],
        out_specs=pl.BlockSpec((bm, d), lambda i: (i, 0)),
        out_shape=jax.ShapeDtypeStruct((n, d), F32),
        compiler_params=_params(("parallel",), 56),
        name="add_transposed",
    )(h, out_t)


def _ple_kernel(h_ref, g_ref, wg_ref, p_ref, wp_ref, hcol_ref, o_ref, xn_ref):
    @pl.when(pl.program_id(1) == 0)
    def _():
        def store(rows, xn):
            xn_ref[rows, :] = xn.astype(BF16)

        _rms_row_groups(h_ref, g_ref, store)

    gate = jax.nn.sigmoid(jnp.dot(xn_ref[...], wg_ref[...], preferred_element_type=F32))
    pp = jnp.dot(p_ref[...], wp_ref[...], preferred_element_type=F32)
    o_ref[...] = hcol_ref[...] + pp * gate


def _ple(h, g, w_gate, p, w_proj):
    n, d = h.shape
    bm = min(1024, n)
    bn = 512
    pd = p.shape[1]
    return pl.pallas_call(
        _ple_kernel,
        grid=(n // bm, d // bn),
        in_specs=[
            pl.BlockSpec((bm, d), lambda i, j: (i, 0), pipeline_mode=pl.Buffered(1)),
            pl.BlockSpec((1, d), lambda i, j: (0, 0)),
            pl.BlockSpec((d, bn), lambda i, j: (0, j)),
            pl.BlockSpec((bm, pd), lambda i, j: (i, 0)),
            pl.BlockSpec((pd, bn), lambda i, j: (0, j)),
            pl.BlockSpec((bm, bn), lambda i, j: (i, j)),
        ],
        out_specs=pl.BlockSpec((bm, bn), lambda i, j: (i, j)),
        out_shape=jax.ShapeDtypeStruct((n, d), F32),
        scratch_shapes=[pltpu.VMEM((bm, d), BF16)],
        compiler_params=_params(("parallel", "arbitrary"), 56),
        name="ple",
    )(h, g, w_gate, p, w_proj, h)


def _final_norm_kernel(h_ref, g_ref, o_ref):
    def store(rows, xn):
        o_ref[rows, :] = xn

    _rms_row_groups(h_ref, g_ref, store)


def _final_norm(h, g):
    n, d = h.shape
    bm = min(512, n)
    return pl.pallas_call(
        _final_norm_kernel,
        grid=(n // bm,),
        in_specs=[pl.BlockSpec((bm, d), lambda i: (i, 0)), pl.BlockSpec((1, d), lambda i: (0, 0))],
        out_specs=pl.BlockSpec((bm, d), lambda i: (i, 0)),
        out_shape=jax.ShapeDtypeStruct((n, d), F32),
        compiler_params=_params(("parallel",), 48),
        name="final_norm",
    )(h, g)


def _cast_kernel(x_ref, o_ref):
    o_ref[...] = x_ref[...].astype(o_ref.dtype)


def _cast_layer(x, layer):
    _, r, c = x.shape
    bm = min(CAST_ROWS, r)
    return pl.pallas_call(
        _cast_kernel,
        grid=(r // bm,),
        in_specs=[pl.BlockSpec((None, bm, c), lambda i: (layer, i, 0))],
        out_specs=pl.BlockSpec((bm, c), lambda i: (i, 0)),
        out_shape=jax.ShapeDtypeStruct((r, c), BF16),
        compiler_params=_params(("parallel",), 40),
        name="cast_weight",
    )(x)


def _w_in_t_kernel(a_ref, b_ref, o_ref, lr_ref, *, n_lo):
    k = pl.program_id(0)
    bm = o_ref.shape[0]

    @pl.when(k == n_lo - 1)
    def _():
        lr_ref[...] = jnp.zeros_like(lr_ref)
        lr_ref[0:GLA_LR, :] = b_ref[...].astype(BF16)

    @pl.when(k < n_lo)
    def _():
        o_ref[...] = a_ref[...].astype(BF16)

    @pl.when(k >= n_lo)
    def _():
        o_ref[0:bm - GLA_LR, :] = a_ref[GLA_LR:bm, :].astype(BF16)
        o_ref[bm - GLA_LR:bm, :] = b_ref[...].astype(BF16)


def _split_w_in_t(w_in_t, layer):
    _, cols, d = w_in_t.shape
    lr0 = 2 * (d // 4) + 2 * (d // 2)
    bm = CAST_ROWS
    n_out = cols - GLA_LR
    assert lr0 % bm == 0 and n_out % bm == 0 and bm % GLA_LR == 0
    return pl.pallas_call(
        functools.partial(_w_in_t_kernel, n_lo=lr0 // bm),
        grid=(n_out // bm,),
        in_specs=[pl.BlockSpec((None, bm, d), lambda k: (layer, k, 0)),
                  pl.BlockSpec((None, GLA_LR, d), lambda k: (layer, (k + 1) * (bm // GLA_LR), 0))],
        out_specs=[pl.BlockSpec((bm, d), lambda k: (k, 0)), pl.BlockSpec((LANES, d), lambda k: (0, 0))],
        out_shape=[jax.ShapeDtypeStruct((n_out, d), BF16), jax.ShapeDtypeStruct((LANES, d), BF16)],
        compiler_params=_params(("arbitrary",), 40),
        name="split_w_in",
    )(w_in_t, w_in_t)


def _prep_layer(i, g_mix, w_in, gla_w_lr, gla_b_lr, gla_g_out, conf_w_dw, conf_b_dw, conf_ln_g, conf_ln_b,
                sc_w_dw, w_out, g_ffn, peer_w_q, peer_k1, peer_k2, peer_u, peer_v, g_ple, ple_w_gate, ple_w_proj):
    row = lambda x: x[i][None, :]
    w_main, w_lr_in = _split_w_in_t(jnp.swapaxes(w_in, 1, 2), i)
    return dict(
        g_mix=row(g_mix), w_main=w_main, w_lr_in=w_lr_in,
        w_lr=jnp.pad(gla_w_lr[i], ((0, LANES - GLA_LR), (0, 0))).astype(BF16),
        b_lr=row(gla_b_lr), gg=row(gla_g_out),
        wc=conf_w_dw[i], bc=row(conf_b_dw), lg=row(conf_ln_g), lb=row(conf_ln_b), ws=sc_w_dw[i],
        w_out=_cast_layer(w_out, i),
        g_ffn=row(g_ffn), wq_t=peer_w_q[i].T.astype(BF16),
        k1=peer_k1[i].astype(BF16), k2=peer_k2[i].astype(BF16),
        u=_cast_layer(peer_u, i), v=_cast_layer(peer_v, i),
        g_ple=row(g_ple), w_gate=_cast_layer(ple_w_gate, i), w_proj=ple_w_proj[i].astype(BF16),
    )


def _layer(h, p, s_gla, s_conf, s_sc, lw, batch, seq):
    proj, la = _in_proj(h, lw["g_mix"], lw["w_main"], lw["w_lr_in"], lw["w_lr"], lw["b_lr"])
    y_gla, new_gla = _gla(proj, la, lw["gg"], s_gla, batch, seq)
    y_conf, y_sc, new_conf, new_sc = _convs(proj, s_conf, s_sc, lw["wc"], lw["bc"], lw["lg"], lw["lb"], lw["ws"],
                                            batch, seq)
    h = _out_proj(y_gla, y_conf, y_sc, lw["w_out"], h)
    xnt, s1, s2 = _scores(h, lw["g_ffn"], lw["wq_t"], lw["k1"], lw["k2"])
    c1, e1, r2, e2 = _topk(s1, s2)
    h = _add_transposed(h, _peer_ffn(xnt, lw["u"], lw["v"], c1, e1, r2, e2))
    h = _ple(h, lw["g_ple"], lw["w_gate"], p.astype(BF16), lw["w_proj"])
    return h, new_gla, new_conf, new_sc


def _trunk(x, p, s_gla, s_conf, s_sc, layers, g_final):
    batch, seq, d = x.shape
    h = x.reshape(batch * seq, d)
    new_gla, new_conf, new_sc = [], [], []
    for i, lw in enumerate(layers):
        st = (None, None, None) if s_gla is None else (s_gla[i], s_conf[i], s_sc[i])
        h, sg, scf, ssc = _layer(h, p[i].reshape(batch * seq, -1), *st, lw, batch, seq)
        new_gla.append(sg)
        new_conf.append(scf)
        new_sc.append(ssc)
    y = _final_norm(h, g_final[None, :]).reshape(batch, seq, d)
    return y, jnp.stack(new_gla), jnp.stack(new_conf), jnp.stack(new_sc)


def kernel(x_prompt, x_sample, state_gla, state_conf, state_sconv, p_prompt, p_sample, g_mix, w_in, gla_w_lr, gla_b_lr, gla_g_out, conf_w_dw, conf_b_dw, conf_ln_g, conf_ln_b, sc_w_dw, w_out, g_ffn, peer_w_q, peer_k1, peer_k2, peer_u, peer_v, g_ple, ple_w_gate, ple_w_proj, g_final):
    layers = [_prep_layer(i, g_mix, w_in, gla_w_lr, gla_b_lr, gla_g_out, conf_w_dw, conf_b_dw, conf_ln_g,
                          conf_ln_b, sc_w_dw, w_out, g_ffn, peer_w_q, peer_k1, peer_k2, peer_u, peer_v, g_ple,
                          ple_w_gate, ple_w_proj) for i in range(w_in.shape[0])]
    y_p, gla_p, conf_p, sc_p = _trunk(x_prompt, p_prompt, None, None, None, layers, g_final)
    y_s, gla_s, conf_s, sc_s = _trunk(x_sample, p_sample, state_gla, state_conf, state_sconv, layers, g_final)
    return (y_p, y_s, gla_p, conf_p, sc_p, gla_s, conf_s, sc_s)
```
